```python
import jax, jax.numpy as jnp
from jax import lax
import numpy as np

D_MODEL = 1024
BATCH = 4
SEQ = 8192
DEPTH = 4

CTX_LEN = 256
GRID_W = 64
N_MIXERS = 2
HG_DK = 128
HG_HEADS = D_MODEL // HG_DK
HG_DV = D_MODEL // HG_HEADS
HG_CHUNK = 64
HK = HG_HEADS * HG_DK
HV = HG_HEADS * HG_DV
HG_IN = 3 * HK + 2 * HV
HG_SPLITS = [HK, HK + HV, HK + 2 * HV, 2 * HK + 2 * HV]
SGU_CHUNK = 128
SGU_WIDTH = 3 * D_MODEL
SGU_GROUPS = 8
D_FF = 128 * ((8 * D_MODEL // 3 + 127) // 128)
CONV_W = 3
N_HGRN = (DEPTH + 1) // 2
N_SGU = DEPTH // 2
ALPHA = (2.0 * DEPTH) ** 0.25
BETA = (8.0 * DEPTH) ** -0.25
LN_EPS = 1e-5
RMS_EPS = 1e-6

kernel_name = "hybrid_hgrn2_chunkgmlp_convffn_deepnorm_prefix"


def layer_norm(x, g, b):
    xf = x.astype(jnp.float32)
    mu = jnp.mean(xf, axis=-1, keepdims=True)
    var = jnp.mean(jnp.square(xf - mu), axis=-1, keepdims=True)
    return ((xf - mu) * lax.rsqrt(var + LN_EPS)).astype(x.dtype) * g + b


def modulate(h, shift, scale):
    return h * (1.0 + scale) + shift


def _heads(t, d):
    b, l, _ = t.shape
    return t.reshape(b, l, HG_HEADS, d).transpose(0, 2, 1, 3).astype(jnp.float32)


def _flip(t):
    return jnp.flip(t, axis=2)


def _forget(z_f, lb):
    lb = lb.reshape(HG_HEADS, 1, HG_DK)
    f = lb + (1.0 - lb) * jax.nn.sigmoid(_heads(z_f, HG_DK))
    return 1.0 - f, jnp.log(f)


def gla_scan(q, k, v, logf, s0):
    b, h, l, _ = q.shape
    n = l // HG_CHUNK
    to_chunks = lambda t: jnp.moveaxis(t.reshape(b, h, n, HG_CHUNK, t.shape[-1]), 2, 0)
    mask = jnp.tril(jnp.ones((HG_CHUNK, HG_CHUNK), bool))[:, :, None]

    def step(s, inp):
        qc, kc, vc, gc = inp
        G = jnp.cumsum(gc, axis=2)
        g_end = G[:, :, -1, :]
        o_inter = jnp.einsum('bhtk,bhkv->bhtv', qc * jnp.exp(G), s)
        rel = jnp.where(mask, G[:, :, :, None, :] - G[:, :, None, :, :], -jnp.inf)
        scores = jnp.einsum('bhtk,bhsk,bhtsk->bhts', qc, kc, jnp.exp(rel))
        o = o_inter + jnp.einsum('bhts,bhsv->bhtv', scores, vc)
        s = jnp.exp(g_end)[..., None] * s + jnp.einsum(
            'bhsk,bhsv->bhkv', kc * jnp.exp(g_end[:, :, None, :] - G), vc)
        return s, o

    s, o = lax.scan(step, s0, (to_chunks(q), to_chunks(k), to_chunks(v), to_chunks(logf)))
    return jnp.moveaxis(o, 0, 2).reshape(b, h, l, -1), s


def gla_final_state(k, v, logf):
    G = jnp.cumsum(logf, axis=2)
    return jnp.einsum('bhsk,bhsv->bhkv', k * jnp.exp(G[:, :, -1:, :] - G), v)


def _hg_readout(o, g, norm_w, w_out, dtype):
    o = o * lax.rsqrt(jnp.mean(o * o, axis=-1, keepdims=True) + RMS_EPS) * norm_w.astype(jnp.float32)
    b, h, l, v = o.shape
    o = o.transpose(0, 2, 1, 3).reshape(b, l, h * v).astype(dtype)
    return (o * jax.nn.silu(g)) @ w_out


def hgrn2_mixer(hx, hc, w_in, lb_fwd, lb_bwd, norm_w, w_out, ctx_out):
    dt = hx.dtype
    if ctx_out:
        qc, gc, ic, ffc, fbc = jnp.split(hc @ w_in, HG_SPLITS, axis=-1)
        qc = _heads(jax.nn.silu(qc), HG_DK)
        vc = _heads(ic, HG_DV)
        kfc, lfc = _forget(ffc, lb_fwd)
        kbc, lbc = _forget(fbc, lb_bwd)
        zero = jnp.zeros(qc.shape[:2] + (HG_DK, HG_DV), jnp.float32)
        oc_f, s_f = gla_scan(qc, kfc, vc, lfc, zero)
        oc_b, s_b = gla_scan(_flip(qc), _flip(kbc), _flip(vc), _flip(lbc), zero)
        yc = _hg_readout(oc_f + _flip(oc_b), gc, norm_w, w_out, dt)
    else:
        ic, ffc, fbc = jnp.split(hc @ w_in[:, HK + HV:], [HV, HV + HK], axis=-1)
        vc = _heads(ic, HG_DV)
        kfc, lfc = _forget(ffc, lb_fwd)
        kbc, lbc = _forget(fbc, lb_bwd)
        s_f = gla_final_state(kfc, vc, lfc)
        s_b = gla_final_state(_flip(kbc), _flip(vc), _flip(lbc))
        yc = None
    qx, gx, ix, ffx, fbx = jnp.split(hx @ w_in, HG_SPLITS, axis=-1)
    qx = _heads(jax.nn.silu(qx), HG_DK)
    vx = _heads(ix, HG_DV)
    kfx, lfx = _forget(ffx, lb_fwd)
    kbx, lbx = _forget(fbx, lb_bwd)
    ox_f, _ = gla_scan(qx, kfx, vx, lfx, s_f)
    ox_b, _ = gla_scan(_flip(qx), _flip(kbx), _flip(vx), _flip(lbx), s_b)
    yx = _hg_readout(ox_f + _flip(ox_b), gx, norm_w, w_out, dt)
    return yx, yc


def hgrn_lower_bounds(hg_lb):
    p = jax.nn.softmax(hg_lb.astype(jnp.float32), axis=1)
    return jnp.cumsum(p, axis=1) - p[:, :1]


def chunk_sgu(h, w_in, ln_g, ln_b, w_s, b_s, w_out):
    b, l, _ = h.shape
    n = l // SGU_CHUNK
    z = jax.nn.gelu(h @ w_in, approximate=False)
    u, v = jnp.split(z, 2, axis=-1)
    v = layer_norm(v, ln_g, ln_b)
    v = v.reshape(b, n, SGU_CHUNK, SGU_GROUPS, SGU_WIDTH // SGU_GROUPS)
    v = jnp.einsum('gpq,bnqgc->bnpgc', w_s, v) + b_s.T[:, :, None]
    return (u * v.reshape(b, l, SGU_WIDTH)) @ w_out


def conv_ffn(h, w_up, conv_w, conv_b, w_down, rows, row_len):
    b, l, _ = h.shape
    y = (h @ w_up).reshape(b, rows, row_len, 2 * D_FF)
    yp = jnp.pad(y, ((0, 0), (0, 0), (1, 1), (0, 0)))
    y = yp[:, :, :-2] * conv_w[0] + yp[:, :, 1:-1] * conv_w[1] + yp[:, :, 2:] * conv_w[2] + conv_b
    a, g = jnp.split(y.reshape(b, l, 2 * D_FF), 2, axis=-1)
    return (jax.nn.silu(g) * a) @ w_down


def setup_inputs(seed: int = 0) -> dict:
    key = jax.random.key(seed)
    ks = jax.random.split(key, 24)
    nrm = lambda k, shape, s: jax.random.normal(k, shape, jnp.float32) * s
    D = D_MODEL
    return {
        "x": nrm(ks[0], (BATCH, SEQ, D), 1.0),
        "c": nrm(ks[1], (BATCH, D), 1.0),
        "ctx": nrm(ks[2], (BATCH, CTX_LEN, D), 1.0),
        "c_ctx": nrm(ks[3], (D,), 1.0),
        "ada_w": nrm(ks[4], (DEPTH, D, 6 * D), D ** -0.5),
        "ada_b": nrm(ks[5], (DEPTH, 6 * D), 0.02),
        "ln_g": 1.0 + nrm(ks[6], (DEPTH, 2, D), 0.02),
        "ln_b": nrm(ks[7], (DEPTH, 2, D), 0.02),
        "hg_w_in": nrm(ks[8], (N_HGRN, D, HG_IN), D ** -0.5),
        "hg_lb": nrm(ks[9], (2, N_HGRN, HK), 0.5),
        "hg_norm_w": 1.0 + nrm(ks[10], (N_HGRN, HG_DV), 0.02),
        "hg_w_out": nrm(ks[11], (N_HGRN, HV, D), HV ** -0.5 * BETA),
        "sgu_w_in": nrm(ks[12], (N_SGU, D, 2 * SGU_WIDTH), D ** -0.5),
        "sgu_ln_g": 1.0 + nrm(ks[13], (N_SGU, SGU_WIDTH), 0.02),
        "sgu_ln_b": nrm(ks[14], (N_SGU, SGU_WIDTH), 0.02),
        "sgu_w_s": nrm(ks[15], (N_SGU, SGU_GROUPS, SGU_CHUNK, SGU_CHUNK), SGU_CHUNK ** -0.5),
        "sgu_b_s": 1.0 + nrm(ks[16], (N_SGU, SGU_GROUPS, SGU_CHUNK), 0.02),
        "sgu_w_out": nrm(ks[17], (N_SGU, SGU_WIDTH, D), SGU_WIDTH ** -0.5 * BETA),
        "ffn_w_up": nrm(ks[18], (DEPTH, D, 2 * D_FF), D ** -0.5),
        "ffn_conv_w": nrm(ks[19], (DEPTH, CONV_W, 2 * D_FF), CONV_W ** -0.5),
        "ffn_conv_b": nrm(ks[20], (DEPTH, 2 * D_FF), 0.02),
        "ffn_w_down": nrm(ks[21], (DEPTH, D_FF, D), D_FF ** -0.5 * BETA),
    }


def reference(x, c, ctx, c_ctx, ada_w, ada_b, ln_g, ln_b, hg_w_in, hg_lb, hg_norm_w, hg_w_out,
              sgu_w_in, sgu_ln_g, sgu_ln_b, sgu_w_s, sgu_b_s, sgu_w_out,
              ffn_w_up, ffn_conv_w, ffn_conv_b, ffn_w_down):
    rows = x.shape[1] // GRID_W
    ctx_len = ctx.shape[1]
    lbs = hgrn_lower_bounds(hg_lb)
    sc, scc = jax.nn.silu(c), jax.nn.silu(c_ctx)
    h, hc = x, ctx
    for layer in range(DEPTH):
        kind = layer % N_MIXERS
        idx = layer // N_MIXERS
        ctx_later = any(j % N_MIXERS == 0 for j in range(layer + 1, DEPTH))
        mx = (sc @ ada_w[layer] + ada_b[layer])[:, None, :]
        sh1, sc1, ga1, sh2, sc2, ga2 = jnp.split(mx, 6, axis=-1)
        if kind == 0 or ctx_later:
            mc = scc @ ada_w[layer] + ada_b[layer]
            csh1, csc1, cga1, csh2, csc2, cga2 = jnp.split(mc, 6, axis=-1)
        if kind == 0:
            yx, yc = hgrn2_mixer(modulate(h, sh1, sc1), modulate(hc, csh1, csc1),
                                 hg_w_in[idx], lbs[0, idx], lbs[1, idx],
                                 hg_norm_w[idx], hg_w_out[idx], ctx_later)
        else:
            yx = chunk_sgu(modulate(h, sh1, sc1), sgu_w_in[idx], sgu_ln_g[idx], sgu_ln_b[idx],
                           sgu_w_s[idx], sgu_b_s[idx], sgu_w_out[idx])
            if ctx_later:
                yc = chunk_sgu(modulate(hc, csh1, csc1), sgu_w_in[idx], sgu_ln_g[idx], sgu_ln_b[idx],
                               sgu_w_s[idx], sgu_b_s[idx], sgu_w_out[idx])
        h = layer_norm(ALPHA * h + ga1 * yx, ln_g[layer, 0], ln_b[layer, 0])
        f = conv_ffn(modulate(h, sh2, sc2), ffn_w_up[layer], ffn_conv_w[layer], ffn_conv_b[layer],
                     ffn_w_down[layer], rows, GRID_W)
        h = layer_norm(ALPHA * h + ga2 * f, ln_g[layer, 1], ln_b[layer, 1])
        if ctx_later:
            hc = layer_norm(ALPHA * hc + cga1 * yc, ln_g[layer, 0], ln_b[layer, 0])
            fc = conv_ffn(modulate(hc, csh2, csc2), ffn_w_up[layer], ffn_conv_w[layer],
                          ffn_conv_b[layer], ffn_w_down[layer], 1, ctx_len)
            hc = layer_norm(ALPHA * hc + cga2 * fc, ln_g[layer, 1], ln_b[layer, 1])
    return h
```

```python
import functools

import jax
import jax.numpy as jnp
from jax import lax
from jax.experimental import pallas as pl
from jax.experimental.pallas import tpu as pltpu

F32 = jnp.float32
BF16 = jnp.bfloat16

LANES = 128
D_MODEL = 1024
DEPTH = 4
N_MIXERS = 2
GRID_W = 64
HG_DK = LANES
HG_HEADS = D_MODEL // HG_DK
SCAN_CHUNK = 64
SGU_CHUNK = 128
SGU_WIDTH = 3 * D_MODEL
SGU_GROUPS = 8
SGU_GW = SGU_WIDTH // SGU_GROUPS
D_FF = 128 * ((8 * D_MODEL // 3 + 127) // 128)
FF_TILE = 256
ALPHA = (2.0 * DEPTH) ** 0.25
LN_EPS = 1e-5
RMS_EPS = 1e-6
VMEM_LIMIT_BYTES = 56 * 1024 * 1024
ADA_TILE = 1536
ADA_ROWS = 8

_NT = (((1,), (1,)), ((), ()))
_TN = (((0,), (0,)), ((), ()))


def _params(*sem):
    return pltpu.CompilerParams(dimension_semantics=sem, vmem_limit_bytes=VMEM_LIMIT_BYTES)


def _resident(shape):
    zeros = (0,) * len(shape)
    return pl.BlockSpec(shape, lambda *_: zeros, pipeline_mode=pl.Buffered(1))


def _per_batch(shape):
    zeros = (0,) * len(shape)
    return pl.BlockSpec((1,) + tuple(shape), lambda b, j: (b,) + zeros)


def _dot(a, b):
    return jnp.dot(a, b, preferred_element_type=F32)


def _silu(x):
    return x * jax.nn.sigmoid(x)


def _gelu(x):
    return 0.5 * x * (1.0 + lax.erf(x * (2.0 ** -0.5)))


def _layer_norm(x, g, b):
    mu = jnp.mean(x, axis=-1, keepdims=True)
    xc = x - mu
    var = jnp.mean(xc * xc, axis=-1, keepdims=True)
    return xc * lax.rsqrt(var + LN_EPS) * g + b


def _modulated_bf16(h, sh_ref, sc_ref):
    return (h * (1.0 + sc_ref[0]) + sh_ref[0]).astype(BF16)


def _ada_kernel(c_ref, w_ref, b_ref, o_ref):
    s = _silu(c_ref[...])
    o_ref[0] = jnp.dot(s, w_ref[0], precision=lax.Precision.HIGHEST,
                       preferred_element_type=F32) + b_ref[0]


def _ada_modulation(cc, ada_w, ada_b):
    n = ada_w.shape[-1]
    return pl.pallas_call(
        _ada_kernel,
        grid=(DEPTH, n // ADA_TILE),
        in_specs=[pl.BlockSpec((ADA_ROWS, D_MODEL), lambda l, j: (0, 0)),
                  pl.BlockSpec((1, D_MODEL, ADA_TILE), lambda l, j: (l, 0, j)),
                  pl.BlockSpec((1, 1, ADA_TILE), lambda l, j: (l, 0, j))],
        out_specs=pl.BlockSpec((1, ADA_ROWS, ADA_TILE), lambda l, j: (l, 0, j)),
        out_shape=jax.ShapeDtypeStruct((DEPTH, ADA_ROWS, n), F32),
        compiler_params=_params("arbitrary", "arbitrary"),
        name="ada_modulation",
    )(cc, ada_w, ada_b.reshape(DEPTH, 1, n))


def _ffn_kernel(h_ref, sh_ref, sc_ref, ga_ref, wa_ref, wg_ref, cwa_ref, cwg_ref, cba_ref, cbg_ref,
                wd_ref, lng_ref, lnb_ref, o_ref, act_ref, *, row_len):
    h = h_ref[0]
    tb = h.shape[0]
    hm = _modulated_bf16(h, sh_ref, sc_ref)
    col = lax.broadcasted_iota(jnp.int32, (tb, FF_TILE), 0) % row_len
    has_prev = col != 0
    has_next = col != row_len - 1

    def conv(y, cw_ref, cb_ref, sl):
        y_prev = jnp.where(has_prev, pltpu.roll(y, 1, 0), 0.0)
        y_next = jnp.where(has_next, pltpu.roll(y, tb - 1, 0), 0.0)
        return (y_prev * cw_ref[0:1, sl] + y * cw_ref[1:2, sl] + y_next * cw_ref[2:3, sl]
                + cb_ref[:, sl])

    for j in range(D_FF // FF_TILE):
        sl = slice(j * FF_TILE, (j + 1) * FF_TILE)
        a = conv(_dot(hm, wa_ref[:, sl]), cwa_ref, cba_ref, sl)
        g = conv(_dot(hm, wg_ref[:, sl]), cwg_ref, cbg_ref, sl)
        act_ref[:, sl] = (_silu(g) * a).astype(BF16)
    f = _dot(act_ref[...], wd_ref[...])
    o_ref[0] = _layer_norm(ALPHA * h + ga_ref[0] * f, lng_ref[...], lnb_ref[...])


def _ffn_layer(h, mods, w, row_len, tb):
    b, l, d = h.shape
    tok = pl.BlockSpec((1, tb, d), lambda i, j: (i, j, 0))
    vec = _per_batch((1, d))
    return pl.pallas_call(
        functools.partial(_ffn_kernel, row_len=row_len),
        grid=(b, l // tb),
        in_specs=[tok, vec, vec, vec,
                  _resident((d, D_FF)), _resident((d, D_FF)),
                  _resident((3, D_FF)), _resident((3, D_FF)),
                  _resident((1, D_FF)), _resident((1, D_FF)),
                  _resident((D_FF, d)), _resident((1, d)), _resident((1, d))],
        out_specs=tok,
        out_shape=jax.ShapeDtypeStruct(h.shape, F32),
        scratch_shapes=[pltpu.VMEM((tb, D_FF), BF16)],
        compiler_params=_params("arbitrary", "arbitrary"),
        name="conv_ffn",
    )(h, mods["sh2"], mods["sc2"], mods["ga2"], w["wa"], w["wg"], w["cwa"], w["cwg"],
      w["cba"], w["cbg"], w["wd"], w["ln_g"], w["ln_b"])


def _sgu_kernel(h_ref, sh_ref, sc_ref, ga_ref, win_ref, vg_ref, vb_ref, ws_ref, bs_ref, wout_ref,
                lng_ref, lnb_ref, o_ref, v_ref, p_ref):
    h = h_ref[0]
    tb = h.shape[0]
    hm = _modulated_bf16(h, sh_ref, sc_ref)
    groups = [slice(g * SGU_GW, (g + 1) * SGU_GW) for g in range(SGU_GROUPS)]

    total = jnp.zeros((tb, 1), F32)
    for sl in groups:
        z = _gelu(_dot(hm, win_ref[:, SGU_WIDTH + sl.start:SGU_WIDTH + sl.stop]))
        v_ref[:, sl] = z
        total = total + jnp.sum(z, axis=-1, keepdims=True)
    mu = total * (1.0 / SGU_WIDTH)
    sq = jnp.zeros((tb, 1), F32)
    for sl in groups:
        zc = v_ref[:, sl] - mu
        sq = sq + jnp.sum(zc * zc, axis=-1, keepdims=True)
    rstd = lax.rsqrt(sq * (1.0 / SGU_WIDTH) + LN_EPS)

    for g, sl in enumerate(groups):
        vn = ((v_ref[:, sl] - mu) * rstd * vg_ref[:, sl] + vb_ref[:, sl]).astype(BF16)
        u = _gelu(_dot(hm, win_ref[:, sl]))
        for n in range(tb // SGU_CHUNK):
            rows = slice(n * SGU_CHUNK, (n + 1) * SGU_CHUNK)
            mixed = _dot(ws_ref[g], vn[rows]) + bs_ref[:, sl]
            p_ref[rows, sl] = (u[rows] * mixed).astype(BF16)
    y = _dot(p_ref[...], wout_ref[...])
    o_ref[0] = _layer_norm(ALPHA * h + ga_ref[0] * y, lng_ref[...], lnb_ref[...])


def _sgu_layer(h, mods, w, tb):
    b, l, d = h.shape
    tok = pl.BlockSpec((1, tb, d), lambda i, j: (i, j, 0))
    vec = _per_batch((1, d))
    return pl.pallas_call(
        _sgu_kernel,
        grid=(b, l // tb),
        in_specs=[tok, vec, vec, vec,
                  _resident((d, 2 * SGU_WIDTH)), _resident((1, SGU_WIDTH)), _resident((1, SGU_WIDTH)),
                  _resident((SGU_GROUPS, SGU_CHUNK, SGU_CHUNK)), _resident((SGU_CHUNK, SGU_WIDTH)),
                  _resident((SGU_WIDTH, d)), _resident((1, d)), _resident((1, d))],
        out_specs=tok,
        out_shape=jax.ShapeDtypeStruct(h.shape, F32),
        scratch_shapes=[pltpu.VMEM((tb, SGU_WIDTH), F32), pltpu.VMEM((tb, SGU_WIDTH), BF16)],
        compiler_params=_params("arbitrary", "arbitrary"),
        name="chunk_sgu",
    )(h, mods["sh1"], mods["sc1"], mods["ga1"], w["w_in"], w["v_g"], w["v_b"], w["w_s"], w["b_s"],
      w["w_out"], w["ln_g"], w["ln_b"])


def _lower_bound(lb_ref, idx):
    x = lb_ref[...]
    e = jnp.exp(x - jnp.max(x, axis=0, keepdims=True))
    p = e / jnp.sum(e, axis=0, keepdims=True)
    acc = p[0:1]
    for r in range(1, idx + 1):
        acc = acc + p[r:r + 1]
    return acc - p[0:1]


def _chunk_cumsum(x, row, rev):
    n = x.shape[0]
    sh = 1
    while sh < n:
        if rev:
            x = x + jnp.where(row < n - sh, pltpu.roll(x, n - sh, 0), 0.0)
        else:
            x = x + jnp.where(row >= sh, pltpu.roll(x, sh, 0), 0.0)
        sh *= 2
    return x


def _block_row(x, row, bs, r):
    n = x.shape[0]
    if bs >= 8:
        parts = [jnp.broadcast_to(x[i * bs + r:i * bs + r + 1, :], (bs, x.shape[1]))
                 for i in range(n // bs)]
        return parts[0] if len(parts) == 1 else jnp.concatenate(parts, axis=0)
    off = row & (bs - 1)
    out = x
    for o in range(bs):
        if o != r:
            out = jnp.where(off == o, pltpu.roll(x, (o - r) % n, 0), out)
    return out


def _scan_consts(rev):
    c = SCAN_CHUNK
    row = lax.broadcasted_iota(jnp.int32, (c, LANES), 0)
    t = lax.broadcasted_iota(jnp.int32, (c, c), 0)
    s = lax.broadcasted_iota(jnp.int32, (c, c), 1)
    x = t ^ s
    causal = (t < s) if rev else (t > s)
    levels = []
    m = c // 2
    while m >= 1:
        levels.append((m, causal & (x >= m) & (x < 2 * m)))
        m //= 2
    return row, t == s, levels


def _scan_chunk(q, k, v, lg, st, rev, consts):
    row, eye, levels = consts
    c = SCAN_CHUNK
    gsum = _chunk_cumsum(lg, row, rev)
    scores = jnp.where(eye, jnp.sum(q * k, axis=-1, keepdims=True), 0.0)
    for m, mask in levels:
        upper = (row & m) != 0
        q_side = jnp.logical_not(upper) if rev else upper
        ref = _block_row(gsum, row, 2 * m, m if rev else m - 1)
        d = jnp.where(q_side, gsum - ref, ref - gsum)
        x = (jnp.where(q_side, q, k) * jnp.exp(d)).astype(BF16)
        s = lax.dot_general(x, x, _NT, preferred_element_type=F32)
        scores = scores + jnp.where(mask, s, 0.0)
    vb = v.astype(BF16)
    o = _dot(scores.astype(BF16), vb)
    qe = (q * jnp.exp(gsum)).astype(BF16)
    o = o + lax.dot_general(qe, st.astype(BF16), _NT, preferred_element_type=F32)
    g_end = gsum[0:1] if rev else gsum[c - 1:c]
    kd = (k * jnp.exp(g_end - gsum)).astype(BF16)
    st_new = st * jnp.exp(g_end) + lax.dot_general(vb, kd, _TN, preferred_element_type=F32)
    return o, st_new


def _forget_gate(z, lb, k_ref, lg_ref):
    f = lb + (1.0 - lb) * jax.nn.sigmoid(z)
    k_ref[...] = 1.0 - f
    lg_ref[...] = jnp.log(f)


def _hg_fwd_kernel(h_ref, sh_ref, sc_ref, wq_ref, wi_ref, wf_ref, lb_ref, s0_ref,
                   o_ref, q_ref, v_ref, st_ref, k_scr, lg_scr, *, lb_idx):
    @pl.when(pl.program_id(1) == 0)
    def _():
        st_ref[...] = s0_ref[...]

    h = h_ref[0]
    tb = h.shape[0]
    hm = _modulated_bf16(h, sh_ref, sc_ref)
    q_ref[0] = _silu(_dot(hm, wq_ref[...]))
    v_ref[0] = _dot(hm, wi_ref[...])
    _forget_gate(_dot(hm, wf_ref[...]), _lower_bound(lb_ref, lb_idx), k_scr, lg_scr)
    consts = _scan_consts(False)

    def chunk(ci, carry):
        rows = pl.ds(pl.multiple_of(ci * SCAN_CHUNK, SCAN_CHUNK), SCAN_CHUNK)
        for hd in range(HG_HEADS):
            cols = slice(hd * HG_DK, (hd + 1) * HG_DK)
            o, st = _scan_chunk(q_ref[0, rows, cols], k_scr[rows, cols], v_ref[0, rows, cols],
                                lg_scr[rows, cols], st_ref[0, hd], False, consts)
            o_ref[0, rows, cols] = o
            st_ref[0, hd] = st
        return carry

    lax.fori_loop(0, tb // SCAN_CHUNK, chunk, 0)


def _hg_bwd_kernel(h_ref, sh_ref, sc_ref, ga_ref, q_ref, v_ref, of_ref, wg_ref, wf_ref, lb_ref,
                   nw_ref, wo_ref, lng_ref, lnb_ref, s0_ref,
                   hn_ref, st_ref, k_scr, lg_scr, g_scr, y_scr, *, lb_idx):
    @pl.when(pl.program_id(1) == 0)
    def _():
        st_ref[...] = s0_ref[...]

    h = h_ref[0]
    tb = h.shape[0]
    n_chunks = tb // SCAN_CHUNK
    hm = _modulated_bf16(h, sh_ref, sc_ref)
    g_scr[...] = _silu(_dot(hm, wg_ref[...]))
    _forget_gate(_dot(hm, wf_ref[...]), _lower_bound(lb_ref, lb_idx), k_scr, lg_scr)
    consts = _scan_consts(True)
    norm_w = nw_ref[...]

    def chunk(ci, carry):
        rows = pl.ds(pl.multiple_of((n_chunks - 1 - ci) * SCAN_CHUNK, SCAN_CHUNK), SCAN_CHUNK)
        for hd in range(HG_HEADS):
            cols = slice(hd * HG_DK, (hd + 1) * HG_DK)
            o, st = _scan_chunk(q_ref[0, rows, cols], k_scr[rows, cols], v_ref[0, rows, cols],
                                lg_scr[rows, cols], st_ref[0, hd], True, consts)
            st_ref[0, hd] = st
            o = o + of_ref[0, rows, cols]
            o = o * lax.rsqrt(jnp.mean(o * o, axis=-1, keepdims=True) + RMS_EPS) * norm_w
            y_scr[rows, cols] = (o * g_scr[rows, cols]).astype(BF16)
        return carry

    lax.fori_loop(0, n_chunks, chunk, 0)
    y = _dot(y_scr[...], wo_ref[...])
    hn_ref[0] = _layer_norm(ALPHA * h + ga_ref[0] * y, lng_ref[...], lnb_ref[...])


def _hgrn_layer(h, mods, w, s0_f, s0_b, tb):
    b, l, d = h.shape
    nb = l // tb
    vec = _per_batch((1, d))
    state = _per_batch((HG_HEADS, HG_DK, HG_DK))
    state_shape = jax.ShapeDtypeStruct((b, HG_HEADS, HG_DK, HG_DK), F32)
    tok_shape = jax.ShapeDtypeStruct(h.shape, F32)
    n_lb = w["lb_f"].shape[0]

    tok = pl.BlockSpec((1, tb, d), lambda i, j: (i, j, 0))
    o_f, q, v, s_f = pl.pallas_call(
        functools.partial(_hg_fwd_kernel, lb_idx=w["lb_idx"]),
        grid=(b, nb),
        in_specs=[tok, vec, vec, _resident((d, d)), _resident((d, d)), _resident((d, d)),
                  _resident((n_lb, d)), state],
        out_specs=[tok, tok, tok, state],
        out_shape=[tok_shape, tok_shape, tok_shape, state_shape],
        scratch_shapes=[pltpu.VMEM((tb, d), F32), pltpu.VMEM((tb, d), F32)],
        compiler_params=_params("arbitrary", "arbitrary"),
        name="hgrn2_forward_scan",
    )(h, mods["sh1"], mods["sc1"], w["wq"], w["wi"], w["wff"], w["lb_f"], s0_f)

    rtok = pl.BlockSpec((1, tb, d), lambda i, j: (i, nb - 1 - j, 0))
    h_new, s_b = pl.pallas_call(
        functools.partial(_hg_bwd_kernel, lb_idx=w["lb_idx"]),
        grid=(b, nb),
        in_specs=[rtok, vec, vec, vec, rtok, rtok, rtok,
                  _resident((d, d)), _resident((d, d)), _resident((n_lb, d)),
                  _resident((1, HG_DK)), _resident((d, d)), _resident((1, d)), _resident((1, d)),
                  state],
        out_specs=[rtok, state],
        out_shape=[tok_shape, state_shape],
        scratch_shapes=[pltpu.VMEM((tb, d), F32), pltpu.VMEM((tb, d), F32),
                        pltpu.VMEM((tb, d), F32), pltpu.VMEM((tb, d), BF16)],
        compiler_params=_params("arbitrary", "arbitrary"),
        name="hgrn2_backward_scan_readout",
    )(h, mods["sh1"], mods["sc1"], mods["ga1"], q, v, o_f, w["wg"], w["wfb"], w["lb_b"],
      w["norm_w"], w["w_out"], w["ln_g"], w["ln_b"], s0_b)
    return h_new, s_f, s_b


def _token_block(l):
    for tb in (512, 256, 128):
        if l % tb == 0:
            return tb
    raise ValueError(f"sequence length {l} is not a multiple of 128")


def kernel(x, c, ctx, c_ctx, ada_w, ada_b, ln_g, ln_b, hg_w_in, hg_lb, hg_norm_w, hg_w_out, sgu_w_in, sgu_ln_g, sgu_ln_b, sgu_w_s, sgu_b_s, sgu_w_out, ffn_w_up, ffn_conv_w, ffn_conv_b, ffn_w_down):
    batch, seq, d = x.shape
    ctx_len = ctx.shape[1]
    assert d == D_MODEL and batch + 1 <= ADA_ROWS
    assert seq % GRID_W == 0 and seq % SGU_CHUNK == 0 and ctx_len % SGU_CHUNK == 0
    tb_x, tb_c = _token_block(seq), _token_block(ctx_len)
    assert tb_x % GRID_W == 0 and tb_c == ctx_len or ctx_len % tb_c == 0

    cc = jnp.concatenate([c, c_ctx[None], jnp.zeros((ADA_ROWS - batch - 1, d), F32)], axis=0)
    ada = _ada_modulation(cc, ada_w, ada_b)
    names = ("sh1", "sc1", "ga1", "sh2", "sc2", "ga2")

    def mods_for(layer, is_ctx):
        m = ada[layer].reshape(ADA_ROWS, 6, 1, d)
        if is_ctx:
            rows = jnp.broadcast_to(m[batch:batch + 1], (batch, 6, 1, d))
        else:
            rows = m[:batch]
        return {n: rows[:, i] for i, n in enumerate(names)}

    k1, k2 = HG_HEADS * HG_DK, 2 * HG_HEADS * HG_DK
    zero_state = jnp.zeros((batch, HG_HEADS, HG_DK, HG_DK), F32)
    h, hc = x, ctx
    for layer in range(DEPTH):
        kind, idx = layer % N_MIXERS, layer // N_MIXERS
        ctx_later = any(j % N_MIXERS == 0 for j in range(layer + 1, DEPTH))
        mx = mods_for(layer, False)
        mc = mods_for(layer, True) if (kind == 0 or ctx_later) else None
        ln1 = {"ln_g": ln_g[layer, 0][None], "ln_b": ln_b[layer, 0][None]}
        if kind == 0:
            wi = hg_w_in[idx].astype(BF16)
            w = dict(ln1, wq=wi[:, :k1], wg=wi[:, k1:k2], wi=wi[:, k2:k2 + k1],
                     wff=wi[:, k2 + k1:k2 + 2 * k1], wfb=wi[:, k2 + 2 * k1:],
                     lb_f=hg_lb[0], lb_b=hg_lb[1], lb_idx=idx,
                     norm_w=hg_norm_w[idx][None], w_out=hg_w_out[idx].astype(BF16))
            hc_new, s_f, s_b = _hgrn_layer(hc, mc, w, zero_state, zero_state, tb_c)
            h, _, _ = _hgrn_layer(h, mx, w, s_f, s_b, tb_x)
            if ctx_later:
                hc = hc_new
        else:
            w = dict(ln1, w_in=sgu_w_in[idx].astype(BF16), v_g=sgu_ln_g[idx][None],
                     v_b=sgu_ln_b[idx][None], w_s=sgu_w_s[idx].astype(BF16),
                     b_s=jnp.repeat(sgu_b_s[idx].T, SGU_GW, axis=1),
                     w_out=sgu_w_out[idx].astype(BF16))
            h = _sgu_layer(h, mx, w, tb_x)
            if ctx_later:
                hc = _sgu_layer(hc, mc, w, tb_c)
        up = ffn_w_up[layer].astype(BF16)
        w = dict(wa=up[:, :D_FF], wg=up[:, D_FF:], cwa=ffn_conv_w[layer][:, :D_FF],
                 cwg=ffn_conv_w[layer][:, D_FF:], cba=ffn_conv_b[layer][None, :D_FF],
                 cbg=ffn_conv_b[layer][None, D_FF:], wd=ffn_w_down[layer].astype(BF16),
                 ln_g=ln_g[layer, 1][None], ln_b=ln_b[layer, 1][None])
        h = _ffn_layer(h, mx, w, GRID_W, tb_x)
        if ctx_later:
            hc = _ffn_layer(hc, mc, w, ctx_len, tb_c)
    return h
```

```python
import functools

import jax
import jax.numpy as jnp
import numpy as np
from jax import lax
from jax.experimental import pallas as pl
from jax.experimental.pallas import tpu as pltpu

F32 = jnp.float32
BF16 = jnp.bfloat16

LANES = 128
D_MODEL = 1024
DEPTH = 4
N_MIXERS = 2
GRID_W = 64
HG_DK = LANES
HG_HEADS = D_MODEL // HG_DK
SCAN_CHUNK = 64
SCAN_UNROLL = 2
SCAN_LEVELS = 6
SCAN_SEGMENTS = SCAN_LEVELS + 2
SGU_CHUNK = 128
SGU_WIDTH = 3 * D_MODEL
SGU_GROUPS = 8
SGU_GW = SGU_WIDTH // SGU_GROUPS
D_FF = 128 * ((8 * D_MODEL // 3 + 127) // 128)
FF_TILE = 256
ALPHA = (2.0 * DEPTH) ** 0.25
LN_EPS = 1e-5
RMS_EPS = 1e-6
VMEM_LIMIT_BYTES = 56 * 1024 * 1024
ADA_TILE = 1536
ADA_ROWS = 8

_NT = (((1,), (1,)), ((), ()))
_TN = (((0,), (0,)), ((), ()))


def _params(*sem):
    return pltpu.CompilerParams(dimension_semantics=sem, vmem_limit_bytes=VMEM_LIMIT_BYTES)


def _resident(shape):
    zeros = (0,) * len(shape)
    return pl.BlockSpec(shape, lambda *_: zeros, pipeline_mode=pl.Buffered(1))


def _per_batch(shape):
    zeros = (0,) * len(shape)
    return pl.BlockSpec((1,) + tuple(shape), lambda b, j: (b,) + zeros)


def _dot(a, b):
    return jnp.dot(a, b, preferred_element_type=F32)


def _silu(x):
    return x * jax.nn.sigmoid(x)


def _gelu(x):
    return 0.5 * x * (1.0 + lax.erf(x * (2.0 ** -0.5)))


def _layer_norm(x, g, b):
    mu = jnp.mean(x, axis=-1, keepdims=True)
    xc = x - mu
    var = jnp.mean(xc * xc, axis=-1, keepdims=True)
    return xc * lax.rsqrt(var + LN_EPS) * g + b


def _modulated_bf16(h, sh_ref, sc_ref):
    return (h * (1.0 + sc_ref[0]) + sh_ref[0]).astype(BF16)


def _ada_kernel(c_ref, w_ref, b_ref, o_ref):
    s = _silu(c_ref[...])
    o_ref[0] = jnp.dot(s, w_ref[0], precision=lax.Precision.HIGHEST,
                       preferred_element_type=F32) + b_ref[0]


def _ada_modulation(cc, ada_w, ada_b):
    n = ada_w.shape[-1]
    return pl.pallas_call(
        _ada_kernel,
        grid=(DEPTH, n // ADA_TILE),
        in_specs=[pl.BlockSpec((ADA_ROWS, D_MODEL), lambda l, j: (0, 0)),
                  pl.BlockSpec((1, D_MODEL, ADA_TILE), lambda l, j: (l, 0, j)),
                  pl.BlockSpec((1, 1, ADA_TILE), lambda l, j: (l, 0, j))],
        out_specs=pl.BlockSpec((1, ADA_ROWS, ADA_TILE), lambda l, j: (l, 0, j)),
        out_shape=jax.ShapeDtypeStruct((DEPTH, ADA_ROWS, n), F32),
        compiler_params=_params("arbitrary", "arbitrary"),
        name="ada_modulation",
    )(cc, ada_w, ada_b.reshape(DEPTH, 1, n))


def _ffn_kernel(h_ref, sh_ref, sc_ref, ga_ref, wa_ref, wg_ref, cwa_ref, cwg_ref, cba_ref, cbg_ref,
                wd_ref, lng_ref, lnb_ref, o_ref, act_ref, *, row_len):
    h = h_ref[0]
    tb = h.shape[0]
    hm = _modulated_bf16(h, sh_ref, sc_ref)
    col = lax.broadcasted_iota(jnp.int32, (tb, FF_TILE), 0) % row_len
    has_prev = col != 0
    has_next = col != row_len - 1

    def conv(y, cw_ref, cb_ref, sl):
        y_prev = jnp.where(has_prev, pltpu.roll(y, 1, 0), 0.0)
        y_next = jnp.where(has_next, pltpu.roll(y, tb - 1, 0), 0.0)
        return (y_prev * cw_ref[0:1, sl] + y * cw_ref[1:2, sl] + y_next * cw_ref[2:3, sl]
                + cb_ref[:, sl])

    for j in range(D_FF // FF_TILE):
        sl = slice(j * FF_TILE, (j + 1) * FF_TILE)
        a = conv(_dot(hm, wa_ref[:, sl]), cwa_ref, cba_ref, sl)
        g = conv(_dot(hm, wg_ref[:, sl]), cwg_ref, cbg_ref, sl)
        act_ref[:, sl] = (_silu(g) * a).astype(BF16)
    f = _dot(act_ref[...], wd_ref[...])
    o_ref[0] = _layer_norm(ALPHA * h + ga_ref[0] * f, lng_ref[...], lnb_ref[...])


def _ffn_layer(h, mods, w, row_len, tb):
    b, l, d = h.shape
    tok = pl.BlockSpec((1, tb, d), lambda i, j: (i, j, 0))
    vec = _per_batch((1, d))
    return pl.pallas_call(
        functools.partial(_ffn_kernel, row_len=row_len),
        grid=(b, l // tb),
        in_specs=[tok, vec, vec, vec,
                  _resident((d, D_FF)), _resident((d, D_FF)),
                  _resident((3, D_FF)), _resident((3, D_FF)),
                  _resident((1, D_FF)), _resident((1, D_FF)),
                  _resident((D_FF, d)), _resident((1, d)), _resident((1, d))],
        out_specs=tok,
        out_shape=jax.ShapeDtypeStruct(h.shape, F32),
        scratch_shapes=[pltpu.VMEM((tb, D_FF), BF16)],
        compiler_params=_params("arbitrary", "arbitrary"),
        name="conv_ffn",
    )(h, mods["sh2"], mods["sc2"], mods["ga2"], w["wa"], w["wg"], w["cwa"], w["cwg"],
      w["cba"], w["cbg"], w["wd"], w["ln_g"], w["ln_b"])


def _sgu_kernel(h_ref, sh_ref, sc_ref, ga_ref, win_ref, vg_ref, vb_ref, ws_ref, bs_ref, wout_ref,
                lng_ref, lnb_ref, o_ref, v_ref, p_ref):
    h = h_ref[0]
    tb = h.shape[0]
    hm = _modulated_bf16(h, sh_ref, sc_ref)
    groups = [slice(g * SGU_GW, (g + 1) * SGU_GW) for g in range(SGU_GROUPS)]

    total = jnp.zeros((tb, 1), F32)
    for sl in groups:
        z = _gelu(_dot(hm, win_ref[:, SGU_WIDTH + sl.start:SGU_WIDTH + sl.stop]))
        v_ref[:, sl] = z
        total = total + jnp.sum(z, axis=-1, keepdims=True)
    mu = total * (1.0 / SGU_WIDTH)
    sq = jnp.zeros((tb, 1), F32)
    for sl in groups:
        zc = v_ref[:, sl] - mu
        sq = sq + jnp.sum(zc * zc, axis=-1, keepdims=True)
    rstd = lax.rsqrt(sq * (1.0 / SGU_WIDTH) + LN_EPS)

    for g, sl in enumerate(groups):
        vn = ((v_ref[:, sl] - mu) * rstd * vg_ref[:, sl] + vb_ref[:, sl]).astype(BF16)
        u = _gelu(_dot(hm, win_ref[:, sl]))
        for n in range(tb // SGU_CHUNK):
            rows = slice(n * SGU_CHUNK, (n + 1) * SGU_CHUNK)
            mixed = _dot(ws_ref[g], vn[rows]) + bs_ref[:, sl]
            p_ref[rows, sl] = (u[rows] * mixed).astype(BF16)
    y = _dot(p_ref[...], wout_ref[...])
    o_ref[0] = _layer_norm(ALPHA * h + ga_ref[0] * y, lng_ref[...], lnb_ref[...])


def _sgu_layer(h, mods, w, tb):
    b, l, d = h.shape
    tok = pl.BlockSpec((1, tb, d), lambda i, j: (i, j, 0))
    vec = _per_batch((1, d))
    return pl.pallas_call(
        _sgu_kernel,
        grid=(b, l // tb),
        in_specs=[tok, vec, vec, vec,
                  _resident((d, 2 * SGU_WIDTH)), _resident((1, SGU_WIDTH)), _resident((1, SGU_WIDTH)),
                  _resident((SGU_GROUPS, SGU_CHUNK, SGU_CHUNK)), _resident((SGU_CHUNK, SGU_WIDTH)),
                  _resident((SGU_WIDTH, d)), _resident((1, d)), _resident((1, d))],
        out_specs=tok,
        out_shape=jax.ShapeDtypeStruct(h.shape, F32),
        scratch_shapes=[pltpu.VMEM((tb, SGU_WIDTH), F32), pltpu.VMEM((tb, SGU_WIDTH), BF16)],
        compiler_params=_params("arbitrary", "arbitrary"),
        name="chunk_sgu",
    )(h, mods["sh1"], mods["sc1"], mods["ga1"], w["w_in"], w["v_g"], w["v_b"], w["w_s"], w["b_s"],
      w["w_out"], w["ln_g"], w["ln_b"])


def _lower_bound(lb_ref, idx):
    x = lb_ref[...]
    e = jnp.exp(x - jnp.max(x, axis=0, keepdims=True))
    p = e / jnp.sum(e, axis=0, keepdims=True)
    acc = p[0:1]
    for r in range(1, idx + 1):
        acc = acc + p[r:r + 1]
    return acc - p[0:1]


def _segment_matrix(rev):
    c = SCAN_CHUNK
    t = np.arange(c)[:, None]
    r = np.arange(c)[None, :]
    blocks = [r >= t, r < t] if rev else [r <= t, r > t]
    m = c // 2
    while m >= 1:
        upper = (t % (2 * m)) >= m
        if rev:
            ref = t - t % (2 * m) + m
            blocks.append(np.where(upper, (r >= ref) & (r < t), (r >= t) & (r < ref)))
        else:
            ref = t - t % (2 * m) + m - 1
            blocks.append(np.where(upper, (r > ref) & (r <= t), (r > t) & (r <= ref)))
        m //= 2
    assert len(blocks) == SCAN_SEGMENTS
    u = np.concatenate(blocks, axis=0).astype(np.float32)
    return jnp.asarray(np.tile(u, (1, 3)), BF16)


def _pair_owner(rev):
    c = SCAN_CHUNK
    t = lax.broadcasted_iota(jnp.int32, (c, c), 0)
    s = lax.broadcasted_iota(jnp.int32, (c, c), 1)
    x = t ^ s
    owner = jnp.where(t == s, SCAN_LEVELS, -1)
    for i in range(SCAN_LEVELS):
        m = (c // 2) >> i
        owner = jnp.where((x >= m) & (x < 2 * m), i, owner)
    return jnp.where((t <= s) if rev else (t >= s), owner, -1)


def _query_rows(rev):
    row = lax.broadcasted_iota(jnp.int32, (SCAN_CHUNK, D_MODEL), 0)
    masks = []
    for i in range(SCAN_LEVELS):
        upper = (row & ((SCAN_CHUNK // 2) >> i)) != 0
        masks.append(jnp.logical_not(upper) if rev else upper)
    return masks


def _forget_gate(z, lb, k_ref, lg3_ref):
    f = lb + (1.0 - lb) * jax.nn.sigmoid(z)
    k_ref[...] = 1.0 - f
    lg = jnp.log(f)
    hi = lg.astype(BF16)
    rest = lg - hi.astype(F32)
    mid = rest.astype(BF16)
    lg3_ref[0] = hi
    lg3_ref[1] = mid
    lg3_ref[2] = (rest - mid.astype(F32)).astype(BF16)


def _scan_operands(q_ref, k_ref, lg3_ref, u_ref, x_ref, dec_ref, diag_ref, rev):
    c = SCAN_CHUNK
    n_levels = SCAN_LEVELS
    query_rows = _query_rows(rev)
    lane = lax.broadcasted_iota(jnp.int32, (c, LANES), 1)

    def chunk(ci, carry):
        rows = pl.ds(pl.multiple_of(ci * c, c), c)
        lg3 = jnp.concatenate([lg3_ref[0, rows, :], lg3_ref[1, rows, :], lg3_ref[2, rows, :]], axis=0)
        seg = _dot(u_ref[...], lg3)
        q = q_ref[0, rows, :]
        k = k_ref[rows, :]
        for i, q_rows in enumerate(query_rows):
            d = seg[(i + 2) * c:(i + 3) * c]
            x_ref[i, rows, :] = (jnp.where(q_rows, q, k) * jnp.exp(d)).astype(BF16)
        gsum = seg[0:c]
        x_ref[n_levels, rows, :] = (q * jnp.exp(gsum)).astype(BF16)
        x_ref[n_levels + 1, rows, :] = (k * jnp.exp(seg[c:2 * c])).astype(BF16)
        dec_ref[ci] = jnp.exp(gsum[0:1] if rev else gsum[c - 1:c])
        qk = q * k
        diag = jnp.zeros((c, LANES), F32)
        for hd in range(HG_HEADS):
            head_sum = jnp.sum(qk[:, hd * HG_DK:(hd + 1) * HG_DK], axis=-1, keepdims=True)
            diag = jnp.where(lane == hd, head_sum, diag)
        diag_ref[rows, :] = diag
        return carry

    lax.fori_loop(0, q_ref.shape[1] // c, chunk, 0)


def _chunk_scores(x_ref, diag_ref, rows, owner):
    diag = diag_ref[rows, :]
    scores = []
    for hd in range(HG_HEADS):
        cols = slice(hd * HG_DK, (hd + 1) * HG_DK)
        s = jnp.where(owner == SCAN_LEVELS, diag[:, hd:hd + 1], 0.0)
        for i in range(SCAN_LEVELS):
            x = x_ref[i, rows, cols]
            s = jnp.where(owner == i, lax.dot_general(x, x, _NT, preferred_element_type=F32), s)
        scores.append(s.astype(BF16))
    return scores


def _scan_step(x_ref, dec, rows, cols, scores, vb, st):
    o = _dot(scores, vb)
    o = o + lax.dot_general(x_ref[SCAN_LEVELS, rows, cols], st.astype(BF16), _NT,
                            preferred_element_type=F32)
    st_new = st * dec + lax.dot_general(vb, x_ref[SCAN_LEVELS + 1, rows, cols], _TN,
                                        preferred_element_type=F32)
    return o, st_new


def _hg_fwd_kernel(h_ref, sh_ref, sc_ref, wq_ref, wi_ref, wf_ref, lb_ref, u_ref, s0_ref,
                   o_ref, q_ref, v_ref, st_ref, k_scr, lg3_scr, x_scr, dec_scr, diag_scr, *, lb_idx):
    @pl.when(pl.program_id(1) == 0)
    def _():
        st_ref[...] = s0_ref[...]

    h = h_ref[0]
    tb = h.shape[0]
    hm = _modulated_bf16(h, sh_ref, sc_ref)
    q_ref[0] = _silu(_dot(hm, wq_ref[...]))
    v_ref[0] = _dot(hm, wi_ref[...]).astype(BF16)
    _forget_gate(_dot(hm, wf_ref[...]), _lower_bound(lb_ref, lb_idx), k_scr, lg3_scr)
    _scan_operands(q_ref, k_scr, lg3_scr, u_ref, x_scr, dec_scr, diag_scr, False)
    owner = _pair_owner(False)

    def chunk(ci, carry):
        rows = pl.ds(pl.multiple_of(ci * SCAN_CHUNK, SCAN_CHUNK), SCAN_CHUNK)
        dec = dec_scr[ci]
        scores = _chunk_scores(x_scr, diag_scr, rows, owner)
        for hd in range(HG_HEADS):
            cols = slice(hd * HG_DK, (hd + 1) * HG_DK)
            o, st = _scan_step(x_scr, dec[:, cols], rows, cols, scores[hd],
                               v_ref[0, rows, cols], st_ref[0, hd])
            o_ref[0, rows, cols] = o
            st_ref[0, hd] = st
        return carry

    lax.fori_loop(0, tb // SCAN_CHUNK, chunk, 0, unroll=SCAN_UNROLL)


def _hg_bwd_kernel(h_ref, sh_ref, sc_ref, ga_ref, q_ref, v_ref, of_ref, wg_ref, wf_ref, lb_ref,
                   nw_ref, wo_ref, lng_ref, lnb_ref, u_ref, s0_ref,
                   hn_ref, st_ref, k_scr, lg3_scr, x_scr, dec_scr, diag_scr, g_scr, y_scr, *, lb_idx):
    @pl.when(pl.program_id(1) == 0)
    def _():
        st_ref[...] = s0_ref[...]

    h = h_ref[0]
    tb = h.shape[0]
    n_chunks = tb // SCAN_CHUNK
    hm = _modulated_bf16(h, sh_ref, sc_ref)
    g_scr[...] = _silu(_dot(hm, wg_ref[...]))
    _forget_gate(_dot(hm, wf_ref[...]), _lower_bound(lb_ref, lb_idx), k_scr, lg3_scr)
    _scan_operands(q_ref, k_scr, lg3_scr, u_ref, x_scr, dec_scr, diag_scr, True)
    owner = _pair_owner(True)
    norm_w = nw_ref[...]

    def chunk(i, carry):
        ci = n_chunks - 1 - i
        rows = pl.ds(pl.multiple_of(ci * SCAN_CHUNK, SCAN_CHUNK), SCAN_CHUNK)
        dec = dec_scr[ci]
        scores = _chunk_scores(x_scr, diag_scr, rows, owner)
        for hd in range(HG_HEADS):
            cols = slice(hd * HG_DK, (hd + 1) * HG_DK)
            o, st = _scan_step(x_scr, dec[:, cols], rows, cols, scores[hd],
                               v_ref[0, rows, cols], st_ref[0, hd])
            st_ref[0, hd] = st
            o = o + of_ref[0, rows, cols]
            o = o * lax.rsqrt(jnp.mean(o * o, axis=-1, keepdims=True) + RMS_EPS) * norm_w
            y_scr[rows, cols] = (o * g_scr[rows, cols]).astype(BF16)
        return carry

    lax.fori_loop(0, n_chunks, chunk, 0, unroll=SCAN_UNROLL)
    y = _dot(y_scr[...], wo_ref[...])
    hn_ref[0] = _layer_norm(ALPHA * h + ga_ref[0] * y, lng_ref[...], lnb_ref[...])


def _hgrn_layer(h, mods, w, s0_f, s0_b, tb):
    b, l, d = h.shape
    nb = l // tb
    vec = _per_batch((1, d))
    state = _per_batch((HG_HEADS, HG_DK, HG_DK))
    state_shape = jax.ShapeDtypeStruct((b, HG_HEADS, HG_DK, HG_DK), F32)
    tok_shape = jax.ShapeDtypeStruct(h.shape, F32)
    n_lb = w["lb_f"].shape[0]
    seg = _resident((SCAN_SEGMENTS * SCAN_CHUNK, 3 * SCAN_CHUNK))
    scan_scratch = [pltpu.VMEM((tb, d), F32),
                    pltpu.VMEM((3, tb, d), BF16),
                    pltpu.VMEM((SCAN_SEGMENTS, tb, d), BF16),
                    pltpu.VMEM((tb // SCAN_CHUNK, 1, d), F32),
                    pltpu.VMEM((tb, LANES), F32)]

    tok = pl.BlockSpec((1, tb, d), lambda i, j: (i, j, 0))
    o_f, q, v, s_f = pl.pallas_call(
        functools.partial(_hg_fwd_kernel, lb_idx=w["lb_idx"]),
        grid=(b, nb),
        in_specs=[tok, vec, vec, _resident((d, d)), _resident((d, d)), _resident((d, d)),
                  _resident((n_lb, d)), seg, state],
        out_specs=[tok, tok, tok, state],
        out_shape=[tok_shape, tok_shape, jax.ShapeDtypeStruct(h.shape, BF16), state_shape],
        scratch_shapes=scan_scratch,
        compiler_params=_params("arbitrary", "arbitrary"),
        name="hgrn2_forward_scan",
    )(h, mods["sh1"], mods["sc1"], w["wq"], w["wi"], w["wff"], w["lb_f"], _segment_matrix(False),
      s0_f)

    rtok = pl.BlockSpec((1, tb, d), lambda i, j: (i, nb - 1 - j, 0))
    h_new, s_b = pl.pallas_call(
        functools.partial(_hg_bwd_kernel, lb_idx=w["lb_idx"]),
        grid=(b, nb),
        in_specs=[rtok, vec, vec, vec, rtok, rtok, rtok,
                  _resident((d, d)), _resident((d, d)), _resident((n_lb, d)),
                  _resident((1, HG_DK)), _resident((d, d)), _resident((1, d)), _resident((1, d)),
                  seg, state],
        out_specs=[rtok, state],
        out_shape=[tok_shape, state_shape],
        scratch_shapes=scan_scratch + [pltpu.VMEM((tb, d), F32), pltpu.VMEM((tb, d), BF16)],
        compiler_params=_params("arbitrary", "arbitrary"),
        name="hgrn2_backward_scan_readout",
    )(h, mods["sh1"], mods["sc1"], mods["ga1"], q, v, o_f, w["wg"], w["wfb"], w["lb_b"],
      w["norm_w"], w["w_out"], w["ln_g"], w["ln_b"], _segment_matrix(True), s0_b)
    return h_new, s_f, s_b


def _token_block(l):
    for tb in (512, 256, 128):
        if l % tb == 0:
            return tb
    raise ValueError(f"sequence length {l} is not a multiple of 128")


def kernel(x, c, ctx, c_ctx, ada_w, ada_b, ln_g, ln_b, hg_w_in, hg_lb, hg_norm_w, hg_w_out, sgu_w_in, sgu_ln_g, sgu_ln_b, sgu_w_s, sgu_b_s, sgu_w_out, ffn_w_up, ffn_conv_w, ffn_conv_b, ffn_w_down):
    batch, seq, d = x.shape
    ctx_len = ctx.shape[1]
    assert d == D_MODEL and batch + 1 <= ADA_ROWS
    tb_x, tb_c = _token_block(seq), _token_block(ctx_len)
    assert tb_x % GRID_W == 0 and tb_c == ctx_len

    cc = jnp.concatenate([c, c_ctx[None], jnp.zeros((ADA_ROWS - batch - 1, d), F32)], axis=0)
    ada = _ada_modulation(cc, ada_w, ada_b)
    names = ("sh1", "sc1", "ga1", "sh2", "sc2", "ga2")

    def mods_for(layer, is_ctx):
        m = ada[layer].reshape(ADA_ROWS, 6, 1, d)
        if is_ctx:
            rows = jnp.broadcast_to(m[batch:batch + 1], (batch, 6, 1, d))
        else:
            rows = m[:batch]
        return {n: rows[:, i] for i, n in enumerate(names)}

    k1, k2 = HG_HEADS * HG_DK, 2 * HG_HEADS * HG_DK
    zero_state = jnp.zeros((batch, HG_HEADS, HG_DK, HG_DK), F32)
    h, hc = x, ctx
    for layer in range(DEPTH):
        kind, idx = layer % N_MIXERS, layer // N_MIXERS
        ctx_later = any(j % N_MIXERS == 0 for j in range(layer + 1, DEPTH))
        mx = mods_for(layer, False)
        mc = mods_for(layer, True) if (kind == 0 or ctx_later) else None
        ln1 = {"ln_g": ln_g[layer, 0][None], "ln_b": ln_b[layer, 0][None]}
        if kind == 0:
            wi = hg_w_in[idx].astype(BF16)
            w = dict(ln1, wq=wi[:, :k1], wg=wi[:, k1:k2], wi=wi[:, k2:k2 + k1],
                     wff=wi[:, k2 + k1:k2 + 2 * k1], wfb=wi[:, k2 + 2 * k1:],
                     lb_f=hg_lb[0], lb_b=hg_lb[1], lb_idx=idx,
                     norm_w=hg_norm_w[idx][None], w_out=hg_w_out[idx].astype(BF16))
            hc_new, s_f, s_b = _hgrn_layer(hc, mc, w, zero_state, zero_state, tb_c)
            h, _, _ = _hgrn_layer(h, mx, w, s_f, s_b, tb_x)
            if ctx_later:
                hc = hc_new
        else:
            w = dict(ln1, w_in=sgu_w_in[idx].astype(BF16), v_g=sgu_ln_g[idx][None],
                     v_b=sgu_ln_b[idx][None], w_s=sgu_w_s[idx].astype(BF16),
                     b_s=jnp.repeat(sgu_b_s[idx].T, SGU_GW, axis=1),
                     w_out=sgu_w_out[idx].astype(BF16))
            h = _sgu_layer(h, mx, w, tb_x)
            if ctx_later:
                hc = _sgu_layer(hc, mc, w, tb_c)
        up = ffn_w_up[layer].astype(BF16)
        w = dict(wa=up[:, :D_FF], wg=up[:, D_FF:], cwa=ffn_conv_w[layer][:, :D_FF],
                 cwg=ffn_conv_w[layer][:, D_FF:], cba=ffn_conv_b[layer][None, :D_FF],
                 cbg=ffn_conv_b[layer][None, D_FF:], wd=ffn_w_down[layer].astype(BF16),
                 ln_g=ln_g[layer, 1][None], ln_b=ln_b[layer, 1][None])
        h = _ffn_layer(h, mx, w, GRID_W, tb_x)
        if ctx_later:
            hc = _ffn_layer(hc, mc, w, ctx_len, tb_c)
    return h
```

```python
import functools

import jax
import jax.numpy as jnp
import numpy as np
from jax import lax
from jax.experimental import pallas as pl
from jax.experimental.pallas import tpu as pltpu

F32 = jnp.float32
BF16 = jnp.bfloat16

LANES = 128
D_MODEL = 1024
DEPTH = 4
N_MIXERS = 2
GRID_W = 64
HG_DK = LANES
HG_HEADS = D_MODEL // HG_DK
SCAN_CHUNK = 64
SCAN_LEVELS = 6
SUBLANES = 8
SCAN_SEGMENTS = 4
SCAN_OPERANDS = SCAN_LEVELS + 2
LOG2_E = 1.4426950408889634
SGU_CHUNK = 128
SGU_WIDTH = 3 * D_MODEL
SGU_GROUPS = 8
SGU_GW = SGU_WIDTH // SGU_GROUPS
D_FF = 128 * ((8 * D_MODEL // 3 + 127) // 128)
FF_TILE = 256
PROJ_TILE = 256
ALPHA = (2.0 * DEPTH) ** 0.25
LN_EPS = 1e-5
RMS_EPS = 1e-6
VMEM_LIMIT_BYTES = 56 * 1024 * 1024
ADA_TILE = 1536
ADA_ROWS = 8

_NT = (((1,), (1,)), ((), ()))
_TN = (((0,), (0,)), ((), ()))


def _params(*sem):
    return pltpu.CompilerParams(dimension_semantics=sem, vmem_limit_bytes=VMEM_LIMIT_BYTES)


def _resident(shape):
    zeros = (0,) * len(shape)
    return pl.BlockSpec(shape, lambda *_: zeros, pipeline_mode=pl.Buffered(1))


def _per_batch(shape):
    zeros = (0,) * len(shape)
    return pl.BlockSpec((1,) + tuple(shape), lambda b, j: (b,) + zeros)


def _dot(a, b):
    return jnp.dot(a, b, preferred_element_type=F32)


def _silu(x):
    return x * jax.nn.sigmoid(x)


def _gelu(x):
    return 0.5 * x * (1.0 + lax.erf(x * (2.0 ** -0.5)))


def _layer_norm(x, g, b):
    mu = jnp.mean(x, axis=-1, keepdims=True)
    xc = x - mu
    var = jnp.mean(xc * xc, axis=-1, keepdims=True)
    return xc * lax.rsqrt(var + LN_EPS) * g + b


def _modulated_bf16(h, sh_ref, sc_ref):
    return (h * (1.0 + sc_ref[0]) + sh_ref[0]).astype(BF16)


def _ada_kernel(c_ref, w_ref, b_ref, o_ref):
    s = _silu(c_ref[...])
    o_ref[0] = jnp.dot(s, w_ref[0], precision=lax.Precision.HIGHEST,
                       preferred_element_type=F32) + b_ref[0]


def _ada_modulation(cc, ada_w, ada_b):
    n = ada_w.shape[-1]
    return pl.pallas_call(
        _ada_kernel,
        grid=(DEPTH, n // ADA_TILE),
        in_specs=[pl.BlockSpec((ADA_ROWS, D_MODEL), lambda l, j: (0, 0)),
                  pl.BlockSpec((1, D_MODEL, ADA_TILE), lambda l, j: (l, 0, j)),
                  pl.BlockSpec((1, 1, ADA_TILE), lambda l, j: (l, 0, j))],
        out_specs=pl.BlockSpec((1, ADA_ROWS, ADA_TILE), lambda l, j: (l, 0, j)),
        out_shape=jax.ShapeDtypeStruct((DEPTH, ADA_ROWS, n), F32),
        compiler_params=_params("arbitrary", "arbitrary"),
        name="ada_modulation",
    )(cc, ada_w, ada_b.reshape(DEPTH, 1, n))


def _ffn_kernel(h_ref, sh_ref, sc_ref, ga_ref, wa_ref, wg_ref, cwa_ref, cwg_ref, cba_ref, cbg_ref,
                wd_ref, lng_ref, lnb_ref, o_ref, act_ref, *, row_len):
    h = h_ref[0]
    tb = h.shape[0]
    hm = _modulated_bf16(h, sh_ref, sc_ref)
    col = lax.broadcasted_iota(jnp.int32, (tb, FF_TILE), 0) % row_len
    has_prev = col != 0
    has_next = col != row_len - 1

    def conv(y, cw_ref, cb_ref, sl):
        y_prev = jnp.where(has_prev, pltpu.roll(y, 1, 0), 0.0)
        y_next = jnp.where(has_next, pltpu.roll(y, tb - 1, 0), 0.0)
        return (y_prev * cw_ref[0:1, sl] + y * cw_ref[1:2, sl] + y_next * cw_ref[2:3, sl]
                + cb_ref[:, sl])

    for j in range(D_FF // FF_TILE):
        sl = slice(j * FF_TILE, (j + 1) * FF_TILE)
        a = conv(_dot(hm, wa_ref[:, sl]), cwa_ref, cba_ref, sl)
        g = conv(_dot(hm, wg_ref[:, sl]), cwg_ref, cbg_ref, sl)
        act_ref[:, sl] = (_silu(g) * a).astype(BF16)
    f = _dot(act_ref[...], wd_ref[...])
    o_ref[0] = _layer_norm(ALPHA * h + ga_ref[0] * f, lng_ref[...], lnb_ref[...])


def _ffn_layer(h, mods, w, row_len, tb):
    b, l, d = h.shape
    tok = pl.BlockSpec((1, tb, d), lambda i, j: (i, j, 0))
    vec = _per_batch((1, d))
    return pl.pallas_call(
        functools.partial(_ffn_kernel, row_len=row_len),
        grid=(b, l // tb),
        in_specs=[tok, vec, vec, vec,
                  _resident((d, D_FF)), _resident((d, D_FF)),
                  _resident((3, D_FF)), _resident((3, D_FF)),
                  _resident((1, D_FF)), _resident((1, D_FF)),
                  _resident((D_FF, d)), _resident((1, d)), _resident((1, d))],
        out_specs=tok,
        out_shape=jax.ShapeDtypeStruct(h.shape, F32),
        scratch_shapes=[pltpu.VMEM((tb, D_FF), BF16)],
        compiler_params=_params("arbitrary", "arbitrary"),
        name="conv_ffn",
    )(h, mods["sh2"], mods["sc2"], mods["ga2"], w["wa"], w["wg"], w["cwa"], w["cwg"],
      w["cba"], w["cbg"], w["wd"], w["ln_g"], w["ln_b"])


def _sgu_kernel(h_ref, sh_ref, sc_ref, ga_ref, win_ref, vg_ref, vb_ref, ws_ref, bs_ref, wout_ref,
                lng_ref, lnb_ref, o_ref, v_ref, p_ref):
    h = h_ref[0]
    tb = h.shape[0]
    hm = _modulated_bf16(h, sh_ref, sc_ref)
    groups = [slice(g * SGU_GW, (g + 1) * SGU_GW) for g in range(SGU_GROUPS)]

    total = jnp.zeros((tb, 1), F32)
    for sl in groups:
        z = _gelu(_dot(hm, win_ref[:, SGU_WIDTH + sl.start:SGU_WIDTH + sl.stop]))
        v_ref[:, sl] = z
        total = total + jnp.sum(z, axis=-1, keepdims=True)
    mu = total * (1.0 / SGU_WIDTH)
    sq = jnp.zeros((tb, 1), F32)
    for sl in groups:
        zc = v_ref[:, sl] - mu
        sq = sq + jnp.sum(zc * zc, axis=-1, keepdims=True)
    rstd = lax.rsqrt(sq * (1.0 / SGU_WIDTH) + LN_EPS)

    for g, sl in enumerate(groups):
        vn = ((v_ref[:, sl] - mu) * rstd * vg_ref[:, sl] + vb_ref[:, sl]).astype(BF16)
        u = _gelu(_dot(hm, win_ref[:, sl]))
        for n in range(tb // SGU_CHUNK):
            rows = slice(n * SGU_CHUNK, (n + 1) * SGU_CHUNK)
            mixed = _dot(ws_ref[g], vn[rows]) + bs_ref[:, sl]
            p_ref[rows, sl] = (u[rows] * mixed).astype(BF16)
    y = _dot(p_ref[...], wout_ref[...])
    o_ref[0] = _layer_norm(ALPHA * h + ga_ref[0] * y, lng_ref[...], lnb_ref[...])


def _sgu_layer(h, mods, w, tb):
    b, l, d = h.shape
    tok = pl.BlockSpec((1, tb, d), lambda i, j: (i, j, 0))
    vec = _per_batch((1, d))
    return pl.pallas_call(
        _sgu_kernel,
        grid=(b, l // tb),
        in_specs=[tok, vec, vec, vec,
                  _resident((d, 2 * SGU_WIDTH)), _resident((1, SGU_WIDTH)), _resident((1, SGU_WIDTH)),
                  _resident((SGU_GROUPS, SGU_CHUNK, SGU_CHUNK)), _resident((SGU_CHUNK, SGU_WIDTH)),
                  _resident((SGU_WIDTH, d)), _resident((1, d)), _resident((1, d))],
        out_specs=tok,
        out_shape=jax.ShapeDtypeStruct(h.shape, F32),
        scratch_shapes=[pltpu.VMEM((tb, SGU_WIDTH), F32), pltpu.VMEM((tb, SGU_WIDTH), BF16)],
        compiler_params=_params("arbitrary", "arbitrary"),
        name="chunk_sgu",
    )(h, mods["sh1"], mods["sc1"], mods["ga1"], w["w_in"], w["v_g"], w["v_b"], w["w_s"], w["b_s"],
      w["w_out"], w["ln_g"], w["ln_b"])


def _lower_bound(lb_ref, idx):
    x = lb_ref[...]
    e = jnp.exp(x - jnp.max(x, axis=0, keepdims=True))
    p = e / jnp.sum(e, axis=0, keepdims=True)
    acc = p[0:1]
    for r in range(1, idx + 1):
        acc = acc + p[r:r + 1]
    return acc - p[0:1]


def _level_half(i):
    return (SCAN_CHUNK // 2) >> i


def _segment_matrix(rev):
    c = SCAN_CHUNK
    t = np.arange(c)[:, None]
    r = np.arange(c)[None, :]
    blocks = [r >= t] if rev else [r <= t]
    for i in range(SCAN_LEVELS):
        m = _level_half(i)
        if m >= SUBLANES:
            continue
        upper = (t % (2 * m)) >= m
        if rev:
            ref = t - t % (2 * m) + m
            blocks.append(np.where(upper, (r >= ref) & (r < t), (r >= t) & (r < ref)))
        else:
            ref = t - t % (2 * m) + m - 1
            blocks.append(np.where(upper, (r > ref) & (r <= t), (r > t) & (r <= ref)))
    assert len(blocks) == SCAN_SEGMENTS
    u = np.concatenate(blocks, axis=0).astype(np.float32)
    return jnp.asarray(np.tile(u, (1, 3)), BF16)


def _pair_owner(rev):
    c = SCAN_CHUNK
    t = lax.broadcasted_iota(jnp.int32, (c, c), 0)
    s = lax.broadcasted_iota(jnp.int32, (c, c), 1)
    x = t ^ s
    owner = jnp.where(t == s, SCAN_LEVELS, -1)
    for i in range(SCAN_LEVELS):
        m = (c // 2) >> i
        owner = jnp.where((x >= m) & (x < 2 * m), i, owner)
    return jnp.where((t <= s) if rev else (t >= s), owner, -1)


def _query_rows(rev):
    row = lax.broadcasted_iota(jnp.int32, (SCAN_CHUNK, D_MODEL), 0)
    masks = []
    for i in range(SCAN_LEVELS):
        upper = (row & ((SCAN_CHUNK // 2) >> i)) != 0
        masks.append(jnp.logical_not(upper) if rev else upper)
    return masks


def _column_tiles(width):
    return [slice(j, j + PROJ_TILE) for j in range(0, width, PROJ_TILE)]


def _forget_gate(z, lb, k_ref, lg3_ref, sl):
    f = lb + (1.0 - lb) * jax.nn.sigmoid(z)
    k_ref[:, sl] = 1.0 - f
    lg = jnp.log(f) * LOG2_E
    hi = lg.astype(BF16)
    rest = lg - hi.astype(F32)
    mid = rest.astype(BF16)
    lg3_ref[0, :, sl] = hi
    lg3_ref[1, :, sl] = mid
    lg3_ref[2, :, sl] = (rest - mid.astype(F32)).astype(BF16)


def _slab_level(gsum, q, k, m, rev):
    parts = []
    for start in range(0, SCAN_CHUNK, 2 * m):
        lo, hi = slice(start, start + m), slice(start + m, start + 2 * m)
        if rev:
            ref = gsum[start + m:start + m + 1]
            parts += [q[lo] * jnp.exp2(gsum[lo] - ref), k[hi] * jnp.exp2(ref - gsum[hi])]
        else:
            ref = gsum[start + m - 1:start + m]
            parts += [k[lo] * jnp.exp2(ref - gsum[lo]), q[hi] * jnp.exp2(gsum[hi] - ref)]
    return jnp.concatenate(parts, axis=0)


def _scan_operands(ci, q_ref, k_ref, lg3_ref, u_ref, x_ref, dec_ref, diag_ref, rev):
    c = SCAN_CHUNK
    query_rows = _query_rows(rev)
    lane = lax.broadcasted_iota(jnp.int32, (c, LANES), 1)
    rows = _chunk_rows(ci)
    lg3 = jnp.concatenate([lg3_ref[0, rows, :], lg3_ref[1, rows, :], lg3_ref[2, rows, :]], axis=0)
    seg = _dot(u_ref[...], lg3)
    gsum = seg[0:c]
    q = q_ref[0, rows, :]
    k = k_ref[rows, :]
    block = 1
    for i in range(SCAN_LEVELS):
        if _level_half(i) >= SUBLANES:
            x = _slab_level(gsum, q, k, _level_half(i), rev)
        else:
            x = jnp.where(query_rows[i], q, k) * jnp.exp2(seg[block * c:(block + 1) * c])
            block += 1
        x_ref[i, rows, :] = x.astype(BF16)
    g_end = gsum[0:1] if rev else gsum[c - 1:c]
    x_ref[SCAN_LEVELS, rows, :] = (q * jnp.exp2(gsum)).astype(BF16)
    x_ref[SCAN_LEVELS + 1, rows, :] = (k * jnp.exp2(g_end - gsum)).astype(BF16)
    dec_ref[ci] = jnp.exp2(g_end)
    qk = q * k
    diag = jnp.zeros((c, LANES), F32)
    for hd in range(HG_HEADS):
        head_sum = jnp.sum(qk[:, hd * HG_DK:(hd + 1) * HG_DK], axis=-1, keepdims=True)
        diag = jnp.where(lane == hd, head_sum, diag)
    diag_ref[rows, :] = diag


def _chunk_rows(ci):
    return slice(ci * SCAN_CHUNK, (ci + 1) * SCAN_CHUNK)


def _scan_block(n_chunks, rev, operands, step):
    order = list(range(n_chunks))[::-1] if rev else list(range(n_chunks))
    operands(order[0])
    for i, ci in enumerate(order):
        if i + 1 < n_chunks:
            operands(order[i + 1])
        step(ci)


def _chunk_scores(x_ref, diag_ref, rows, owner):
    diag = diag_ref[rows, :]
    scores = []
    for hd in range(HG_HEADS):
        cols = slice(hd * HG_DK, (hd + 1) * HG_DK)
        s = jnp.where(owner == SCAN_LEVELS, diag[:, hd:hd + 1], 0.0)
        for i in range(SCAN_LEVELS):
            x = x_ref[i, rows, cols]
            s = jnp.where(owner == i, lax.dot_general(x, x, _NT, preferred_element_type=F32), s)
        scores.append(s.astype(BF16))
    return scores


def _scan_step(x_ref, dec, rows, cols, scores, vb, st):
    o = _dot(scores, vb) + _dot(x_ref[SCAN_LEVELS, rows, cols], st.astype(BF16))
    dec_rows = jnp.broadcast_to(dec, (HG_DK, HG_DK)).T
    st_new = st * dec_rows + lax.dot_general(x_ref[SCAN_LEVELS + 1, rows, cols], vb, _TN,
                                             preferred_element_type=F32)
    return o, st_new


def _hg_fwd_kernel(h_ref, sh_ref, sc_ref, wq_ref, wi_ref, wf_ref, lb_ref, u_ref, s0_ref,
                   o_ref, q_ref, v_ref, st_ref, k_scr, lg3_scr, x_scr, dec_scr, diag_scr, *, lb_idx):
    @pl.when(pl.program_id(1) == 0)
    def _():
        st_ref[...] = s0_ref[...]

    h = h_ref[0]
    tb = h.shape[0]
    hm = _modulated_bf16(h, sh_ref, sc_ref)
    lb = _lower_bound(lb_ref, lb_idx)
    for sl in _column_tiles(h.shape[1]):
        _forget_gate(_dot(hm, wf_ref[:, sl]), lb[:, sl], k_scr, lg3_scr, sl)
        q_ref[0, :, sl] = _silu(_dot(hm, wq_ref[:, sl]))
        v_ref[0, :, sl] = _dot(hm, wi_ref[:, sl]).astype(BF16)
    owner = _pair_owner(False)

    def operands(ci):
        _scan_operands(ci, q_ref, k_scr, lg3_scr, u_ref, x_scr, dec_scr, diag_scr, False)

    def step(ci):
        rows = _chunk_rows(ci)
        dec = dec_scr[ci]
        scores = _chunk_scores(x_scr, diag_scr, rows, owner)
        for hd in range(HG_HEADS):
            cols = slice(hd * HG_DK, (hd + 1) * HG_DK)
            o, st = _scan_step(x_scr, dec[:, cols], rows, cols, scores[hd],
                               v_ref[0, rows, cols], st_ref[0, hd])
            o_ref[0, rows, cols] = o
            st_ref[0, hd] = st

    _scan_block(tb // SCAN_CHUNK, False, operands, step)


def _hg_bwd_kernel(h_ref, sh_ref, sc_ref, ga_ref, q_ref, v_ref, of_ref, wg_ref, wf_ref, lb_ref,
                   nw_ref, wo_ref, lng_ref, lnb_ref, u_ref, s0_ref,
                   hn_ref, st_ref, k_scr, lg3_scr, x_scr, dec_scr, diag_scr, g_scr, y_scr, *, lb_idx):
    @pl.when(pl.program_id(1) == 0)
    def _():
        st_ref[...] = s0_ref[...]

    h = h_ref[0]
    tb = h.shape[0]
    hm = _modulated_bf16(h, sh_ref, sc_ref)
    lb = _lower_bound(lb_ref, lb_idx)
    for sl in _column_tiles(h.shape[1]):
        _forget_gate(_dot(hm, wf_ref[:, sl]), lb[:, sl], k_scr, lg3_scr, sl)
        g_scr[:, sl] = _silu(_dot(hm, wg_ref[:, sl]))
    owner = _pair_owner(True)
    norm_w = nw_ref[...]

    def operands(ci):
        _scan_operands(ci, q_ref, k_scr, lg3_scr, u_ref, x_scr, dec_scr, diag_scr, True)

    def step(ci):
        rows = _chunk_rows(ci)
        dec = dec_scr[ci]
        scores = _chunk_scores(x_scr, diag_scr, rows, owner)
        for hd in range(HG_HEADS):
            cols = slice(hd * HG_DK, (hd + 1) * HG_DK)
            o, st = _scan_step(x_scr, dec[:, cols], rows, cols, scores[hd],
                               v_ref[0, rows, cols], st_ref[0, hd])
            st_ref[0, hd] = st
            o = o + of_ref[0, rows, cols]
            o = o * lax.rsqrt(jnp.mean(o * o, axis=-1, keepdims=True) + RMS_EPS) * norm_w
            y_scr[rows, cols] = (o * g_scr[rows, cols]).astype(BF16)

    _scan_block(tb // SCAN_CHUNK, True, operands, step)
    y = _dot(y_scr[...], wo_ref[...])
    hn_ref[0] = _layer_norm(ALPHA * h + ga_ref[0] * y, lng_ref[...], lnb_ref[...])


def _hgrn_layer(h, mods, w, s0_f, s0_b, tb):
    b, l, d = h.shape
    nb = l // tb
    vec = _per_batch((1, d))
    state = _per_batch((HG_HEADS, HG_DK, HG_DK))
    state_shape = jax.ShapeDtypeStruct((b, HG_HEADS, HG_DK, HG_DK), F32)
    tok_shape = jax.ShapeDtypeStruct(h.shape, F32)
    n_lb = w["lb_f"].shape[0]
    seg = _resident((SCAN_SEGMENTS * SCAN_CHUNK, 3 * SCAN_CHUNK))
    scan_scratch = [pltpu.VMEM((tb, d), F32),
                    pltpu.VMEM((3, tb, d), BF16),
                    pltpu.VMEM((SCAN_OPERANDS, tb, d), BF16),
                    pltpu.VMEM((tb // SCAN_CHUNK, 1, d), F32),
                    pltpu.VMEM((tb, LANES), F32)]

    tok = pl.BlockSpec((1, tb, d), lambda i, j: (i, j, 0))
    o_f, q, v, s_f = pl.pallas_call(
        functools.partial(_hg_fwd_kernel, lb_idx=w["lb_idx"]),
        grid=(b, nb),
        in_specs=[tok, vec, vec, _resident((d, d)), _resident((d, d)), _resident((d, d)),
                  _resident((n_lb, d)), seg, state],
        out_specs=[tok, tok, tok, state],
        out_shape=[tok_shape, tok_shape, jax.ShapeDtypeStruct(h.shape, BF16), state_shape],
        scratch_shapes=scan_scratch,
        compiler_params=_params("arbitrary", "arbitrary"),
        name="hgrn2_forward_scan",
    )(h, mods["sh1"], mods["sc1"], w["wq"], w["wi"], w["wff"], w["lb_f"], _segment_matrix(False),
      s0_f)

    rtok = pl.BlockSpec((1, tb, d), lambda i, j: (i, nb - 1 - j, 0))
    h_new, s_b = pl.pallas_call(
        functools.partial(_hg_bwd_kernel, lb_idx=w["lb_idx"]),
        grid=(b, nb),
        in_specs=[rtok, vec, vec, vec, rtok, rtok, rtok,
                  _resident((d, d)), _resident((d, d)), _resident((n_lb, d)),
                  _resident((1, HG_DK)), _resident((d, d)), _resident((1, d)), _resident((1, d)),
                  seg, state],
        out_specs=[rtok, state],
        out_shape=[tok_shape, state_shape],
        scratch_shapes=scan_scratch + [pltpu.VMEM((tb, d), F32), pltpu.VMEM((tb, d), BF16)],
        compiler_params=_params("arbitrary", "arbitrary"),
        name="hgrn2_backward_scan_readout",
    )(h, mods["sh1"], mods["sc1"], mods["ga1"], q, v, o_f, w["wg"], w["wfb"], w["lb_b"],
      w["norm_w"], w["w_out"], w["ln_g"], w["ln_b"], _segment_matrix(True), s0_b)
    return h_new, s_f, s_b


def _token_block(l):
    for tb in (512, 256, 128):
        if l % tb == 0:
            return tb
    raise ValueError(f"sequence length {l} is not a multiple of 128")


def kernel(x, c, ctx, c_ctx, ada_w, ada_b, ln_g, ln_b, hg_w_in, hg_lb, hg_norm_w, hg_w_out, sgu_w_in, sgu_ln_g, sgu_ln_b, sgu_w_s, sgu_b_s, sgu_w_out, ffn_w_up, ffn_conv_w, ffn_conv_b, ffn_w_down):
    batch, seq, d = x.shape
    ctx_len = ctx.shape[1]
    assert d == D_MODEL and batch + 1 <= ADA_ROWS
    tb_x, tb_c = _token_block(seq), _token_block(ctx_len)
    assert tb_x % GRID_W == 0 and tb_c == ctx_len

    cc = jnp.concatenate([c, c_ctx[None], jnp.zeros((ADA_ROWS - batch - 1, d), F32)], axis=0)
    ada = _ada_modulation(cc, ada_w, ada_b)
    names = ("sh1", "sc1", "ga1", "sh2", "sc2", "ga2")

    def mods_for(layer, is_ctx):
        m = ada[layer].reshape(ADA_ROWS, 6, 1, d)
        if is_ctx:
            rows = jnp.broadcast_to(m[batch:batch + 1], (batch, 6, 1, d))
        else:
            rows = m[:batch]
        return {n: rows[:, i] for i, n in enumerate(names)}

    k1, k2 = HG_HEADS * HG_DK, 2 * HG_HEADS * HG_DK
    zero_state = jnp.zeros((batch, HG_HEADS, HG_DK, HG_DK), F32)
    h, hc = x, ctx
    for layer in range(DEPTH):
        kind, idx = layer % N_MIXERS, layer // N_MIXERS
        ctx_later = any(j % N_MIXERS == 0 for j in range(layer + 1, DEPTH))
        mx = mods_for(layer, False)
        mc = mods_for(layer, True) if (kind == 0 or ctx_later) else None
        ln1 = {"ln_g": ln_g[layer, 0][None], "ln_b": ln_b[layer, 0][None]}
        if kind == 0:
            wi = hg_w_in[idx].astype(BF16)
            w = dict(ln1, wq=wi[:, :k1], wg=wi[:, k1:k2], wi=wi[:, k2:k2 + k1],
                     wff=wi[:, k2 + k1:k2 + 2 * k1], wfb=wi[:, k2 + 2 * k1:],
                     lb_f=hg_lb[0], lb_b=hg_lb[1], lb_idx=idx,
                     norm_w=hg_norm_w[idx][None], w_out=hg_w_out[idx].astype(BF16))
            hc_new, s_f, s_b = _hgrn_layer(hc, mc, w, zero_state, zero_state, tb_c)
            h, _, _ = _hgrn_layer(h, mx, w, s_f, s_b, tb_x)
            if ctx_later:
                hc = hc_new
        else:
            w = dict(ln1, w_in=sgu_w_in[idx].astype(BF16), v_g=sgu_ln_g[idx][None],
                     v_b=sgu_ln_b[idx][None], w_s=sgu_w_s[idx].astype(BF16),
                     b_s=jnp.repeat(sgu_b_s[idx].T, SGU_GW, axis=1),
                     w_out=sgu_w_out[idx].astype(BF16))
            h = _sgu_layer(h, mx, w, tb_x)
            if ctx_later:
                hc = _sgu_layer(hc, mc, w, tb_c)
        up = ffn_w_up[layer].astype(BF16)
        w = dict(wa=up[:, :D_FF], wg=up[:, D_FF:], cwa=ffn_conv_w[layer][:, :D_FF],
                 cwg=ffn_conv_w[layer][:, D_FF:], cba=ffn_conv_b[layer][None, :D_FF],
                 cbg=ffn_conv_b[layer][None, D_FF:], wd=ffn_w_down[layer].astype(BF16),
                 ln_g=ln_g[layer, 1][None], ln_b=ln_b[layer, 1][None])
        h = _ffn_layer(h, mx, w, GRID_W, tb_x)
        if ctx_later:
            hc = _ffn_layer(hc, mc, w, ctx_len, tb_c)
    return h
```

```python
import functools

import jax
import jax.numpy as jnp
import numpy as np
from jax import lax
from jax.experimental import pallas as pl
from jax.experimental.pallas import tpu as pltpu

F32 = jnp.float32
BF16 = jnp.bfloat16

LANES = 128
D_MODEL = 1024
DEPTH = 4
N_MIXERS = 2
GRID_W = 64
HG_DK = LANES
HG_HEADS = D_MODEL // HG_DK
SCAN_CHUNK = 64
SCAN_LEVELS = 6
SUBLANES = 8
SCAN_SEGMENTS = 4
SCAN_OPERANDS = SCAN_LEVELS + 2
LOG2_E = 1.4426950408889634
SGU_CHUNK = 128
SGU_WIDTH = 3 * D_MODEL
SGU_GROUPS = 8
SGU_GW = SGU_WIDTH // SGU_GROUPS
SGU_TILE = 2 * SGU_GW
D_FF = 128 * ((8 * D_MODEL // 3 + 127) // 128)
FF_TILE = 256
PROJ_TILE = 256
ALPHA = (2.0 * DEPTH) ** 0.25
LN_EPS = 1e-5
RMS_EPS = 1e-6
VMEM_LIMIT_BYTES = 56 * 1024 * 1024
ADA_TILE = 1536
ADA_ROWS = 8

_NT = (((1,), (1,)), ((), ()))
_TN = (((0,), (0,)), ((), ()))


def _params(*sem):
    return pltpu.CompilerParams(dimension_semantics=sem, vmem_limit_bytes=VMEM_LIMIT_BYTES)


def _resident(shape):
    zeros = (0,) * len(shape)
    return pl.BlockSpec(shape, lambda *_: zeros, pipeline_mode=pl.Buffered(1))


def _per_batch(shape):
    zeros = (0,) * len(shape)
    return pl.BlockSpec((1,) + tuple(shape), lambda b, j: (b,) + zeros)


def _dot(a, b):
    return jnp.dot(a, b, preferred_element_type=F32)


def _silu(x):
    return x * jax.nn.sigmoid(x)


def _gelu(x):
    return 0.5 * x * (1.0 + lax.erf(x * (2.0 ** -0.5)))


def _layer_norm(x, g, b):
    mu = jnp.mean(x, axis=-1, keepdims=True)
    xc = x - mu
    var = jnp.mean(xc * xc, axis=-1, keepdims=True)
    return xc * lax.rsqrt(var + LN_EPS) * g + b


def _modulated_bf16(h, sh_ref, sc_ref):
    return (h * (1.0 + sc_ref[0]) + sh_ref[0]).astype(BF16)


def _ada_kernel(c_ref, w_ref, b_ref, o_ref):
    s = _silu(c_ref[...])
    o_ref[0] = jnp.dot(s, w_ref[0], precision=lax.Precision.HIGHEST,
                       preferred_element_type=F32) + b_ref[0]


def _ada_modulation(cc, ada_w, ada_b):
    n = ada_w.shape[-1]
    return pl.pallas_call(
        _ada_kernel,
        grid=(DEPTH, n // ADA_TILE),
        in_specs=[pl.BlockSpec((ADA_ROWS, D_MODEL), lambda l, j: (0, 0)),
                  pl.BlockSpec((1, D_MODEL, ADA_TILE), lambda l, j: (l, 0, j)),
                  pl.BlockSpec((1, 1, ADA_TILE), lambda l, j: (l, 0, j))],
        out_specs=pl.BlockSpec((1, ADA_ROWS, ADA_TILE), lambda l, j: (l, 0, j)),
        out_shape=jax.ShapeDtypeStruct((DEPTH, ADA_ROWS, n), F32),
        compiler_params=_params("arbitrary", "arbitrary"),
        name="ada_modulation",
    )(cc, ada_w, ada_b.reshape(DEPTH, 1, n))


def _ffn_kernel(h_ref, sh_ref, sc_ref, ga_ref, wa_ref, wg_ref, cwa_ref, cwg_ref, cba_ref, cbg_ref,
                wd_ref, lng_ref, lnb_ref, o_ref, act_ref, *, row_len):
    h = h_ref[0]
    tb = h.shape[0]
    hm = _modulated_bf16(h, sh_ref, sc_ref)
    col = lax.broadcasted_iota(jnp.int32, (tb, FF_TILE), 0) % row_len
    has_prev = col != 0
    has_next = col != row_len - 1

    def conv(y, cw_ref, cb_ref, sl):
        y_prev = jnp.where(has_prev, pltpu.roll(y, 1, 0), 0.0)
        y_next = jnp.where(has_next, pltpu.roll(y, tb - 1, 0), 0.0)
        return (y_prev * cw_ref[0:1, sl] + y * cw_ref[1:2, sl] + y_next * cw_ref[2:3, sl]
                + cb_ref[:, sl])

    for j in range(D_FF // FF_TILE):
        sl = slice(j * FF_TILE, (j + 1) * FF_TILE)
        a = conv(_dot(hm, wa_ref[:, sl]), cwa_ref, cba_ref, sl)
        g = conv(_dot(hm, wg_ref[:, sl]), cwg_ref, cbg_ref, sl)
        act_ref[:, sl] = (_silu(g) * a).astype(BF16)
    f = _dot(act_ref[...], wd_ref[...])
    o_ref[0] = _layer_norm(ALPHA * h + ga_ref[0] * f, lng_ref[...], lnb_ref[...])


def _ffn_layer(h, mods, w, row_len, tb):
    b, l, d = h.shape
    tok = pl.BlockSpec((1, tb, d), lambda i, j: (i, j, 0))
    vec = _per_batch((1, d))
    return pl.pallas_call(
        functools.partial(_ffn_kernel, row_len=row_len),
        grid=(b, l // tb),
        in_specs=[tok, vec, vec, vec,
                  _resident((d, D_FF)), _resident((d, D_FF)),
                  _resident((3, D_FF)), _resident((3, D_FF)),
                  _resident((1, D_FF)), _resident((1, D_FF)),
                  _resident((D_FF, d)), _resident((1, d)), _resident((1, d))],
        out_specs=tok,
        out_shape=jax.ShapeDtypeStruct(h.shape, F32),
        scratch_shapes=[pltpu.VMEM((tb, D_FF), BF16)],
        compiler_params=_params("arbitrary", "arbitrary"),
        name="conv_ffn",
    )(h, mods["sh2"], mods["sc2"], mods["ga2"], w["wa"], w["wg"], w["cwa"], w["cwg"],
      w["cba"], w["cbg"], w["wd"], w["ln_g"], w["ln_b"])


def _sgu_kernel(h_ref, sh_ref, sc_ref, ga_ref, win_ref, vg_ref, vb_ref, ws_ref, bs_ref, wout_ref,
                lng_ref, lnb_ref, o_ref, v_ref, p_ref):
    h = h_ref[0]
    tb = h.shape[0]
    hm = _modulated_bf16(h, sh_ref, sc_ref)
    tiles = [slice(j, j + SGU_TILE) for j in range(0, SGU_WIDTH, SGU_TILE)]

    total = jnp.zeros((tb, 1), F32)
    for sl in tiles:
        z = _gelu(_dot(hm, win_ref[:, SGU_WIDTH + sl.start:SGU_WIDTH + sl.stop]))
        v_ref[:, sl] = z
        total = total + jnp.sum(z, axis=-1, keepdims=True)
    mu = total * (1.0 / SGU_WIDTH)
    sq = jnp.zeros((tb, 1), F32)
    for sl in tiles:
        zc = v_ref[:, sl] - mu
        sq = sq + jnp.sum(zc * zc, axis=-1, keepdims=True)
    rstd = lax.rsqrt(sq * (1.0 / SGU_WIDTH) + LN_EPS)

    for t, sl in enumerate(tiles):
        vn = ((v_ref[:, sl] - mu) * rstd * vg_ref[:, sl] + vb_ref[:, sl]).astype(BF16)
        u = _gelu(_dot(hm, win_ref[:, sl]))
        for gi in range(SGU_TILE // SGU_GW):
            g = t * (SGU_TILE // SGU_GW) + gi
            lo = slice(gi * SGU_GW, (gi + 1) * SGU_GW)
            gl = slice(g * SGU_GW, (g + 1) * SGU_GW)
            for n in range(tb // SGU_CHUNK):
                rows = slice(n * SGU_CHUNK, (n + 1) * SGU_CHUNK)
                mixed = _dot(ws_ref[g], vn[rows, lo]) + bs_ref[:, gl]
                p_ref[rows, gl] = (u[rows, lo] * mixed).astype(BF16)
    y = _dot(p_ref[...], wout_ref[...])
    o_ref[0] = _layer_norm(ALPHA * h + ga_ref[0] * y, lng_ref[...], lnb_ref[...])


def _sgu_layer(h, mods, w, tb):
    b, l, d = h.shape
    tok = pl.BlockSpec((1, tb, d), lambda i, j: (i, j, 0))
    vec = _per_batch((1, d))
    return pl.pallas_call(
        _sgu_kernel,
        grid=(b, l // tb),
        in_specs=[tok, vec, vec, vec,
                  _resident((d, 2 * SGU_WIDTH)), _resident((1, SGU_WIDTH)), _resident((1, SGU_WIDTH)),
                  _resident((SGU_GROUPS, SGU_CHUNK, SGU_CHUNK)), _resident((SGU_CHUNK, SGU_WIDTH)),
                  _resident((SGU_WIDTH, d)), _resident((1, d)), _resident((1, d))],
        out_specs=tok,
        out_shape=jax.ShapeDtypeStruct(h.shape, F32),
        scratch_shapes=[pltpu.VMEM((tb, SGU_WIDTH), F32), pltpu.VMEM((tb, SGU_WIDTH), BF16)],
        compiler_params=_params("arbitrary", "arbitrary"),
        name="chunk_sgu",
    )(h, mods["sh1"], mods["sc1"], mods["ga1"], w["w_in"], w["v_g"], w["v_b"], w["w_s"], w["b_s"],
      w["w_out"], w["ln_g"], w["ln_b"])


def _lower_bound(lb_ref, idx):
    x = lb_ref[...]
    e = jnp.exp(x - jnp.max(x, axis=0, keepdims=True))
    p = e / jnp.sum(e, axis=0, keepdims=True)
    acc = p[0:1]
    for r in range(1, idx + 1):
        acc = acc + p[r:r + 1]
    return acc - p[0:1]


def _level_half(i):
    return (SCAN_CHUNK // 2) >> i


def _segment_matrix(rev):
    c = SCAN_CHUNK
    t = np.arange(c)[:, None]
    r = np.arange(c)[None, :]
    blocks = [r >= t] if rev else [r <= t]
    for i in range(SCAN_LEVELS):
        m = _level_half(i)
        if m >= SUBLANES:
            continue
        upper = (t % (2 * m)) >= m
        if rev:
            ref = t - t % (2 * m) + m
            blocks.append(np.where(upper, (r >= ref) & (r < t), (r >= t) & (r < ref)))
        else:
            ref = t - t % (2 * m) + m - 1
            blocks.append(np.where(upper, (r > ref) & (r <= t), (r > t) & (r <= ref)))
    assert len(blocks) == SCAN_SEGMENTS
    u = np.concatenate(blocks, axis=0).astype(np.float32)
    return jnp.asarray(np.tile(u, (1, 3)), BF16)


def _pair_owner(rev):
    c = SCAN_CHUNK
    t = lax.broadcasted_iota(jnp.int32, (c, c), 0)
    s = lax.broadcasted_iota(jnp.int32, (c, c), 1)
    x = t ^ s
    owner = jnp.where(t == s, SCAN_LEVELS, -1)
    for i in range(SCAN_LEVELS):
        m = (c // 2) >> i
        owner = jnp.where((x >= m) & (x < 2 * m), i, owner)
    return jnp.where((t <= s) if rev else (t >= s), owner, -1)


def _query_rows(rev):
    row = lax.broadcasted_iota(jnp.int32, (SCAN_CHUNK, D_MODEL), 0)
    masks = []
    for i in range(SCAN_LEVELS):
        upper = (row & ((SCAN_CHUNK // 2) >> i)) != 0
        masks.append(jnp.logical_not(upper) if rev else upper)
    return masks


def _column_tiles(width):
    return [slice(j, j + PROJ_TILE) for j in range(0, width, PROJ_TILE)]


def _forget_gate(z, lb, k_ref, lg3_ref, sl):
    f = lb + (1.0 - lb) * jax.nn.sigmoid(z)
    k_ref[:, sl] = 1.0 - f
    lg = jnp.log(f) * LOG2_E
    hi = lg.astype(BF16)
    rest = lg - hi.astype(F32)
    mid = rest.astype(BF16)
    lg3_ref[0, :, sl] = hi
    lg3_ref[1, :, sl] = mid
    lg3_ref[2, :, sl] = (rest - mid.astype(F32)).astype(BF16)


def _slab_level(gsum, q, k, m, rev):
    parts = []
    for start in range(0, SCAN_CHUNK, 2 * m):
        lo, hi = slice(start, start + m), slice(start + m, start + 2 * m)
        if rev:
            ref = gsum[start + m:start + m + 1]
            parts += [q[lo] * jnp.exp2(gsum[lo] - ref), k[hi] * jnp.exp2(ref - gsum[hi])]
        else:
            ref = gsum[start + m - 1:start + m]
            parts += [k[lo] * jnp.exp2(ref - gsum[lo]), q[hi] * jnp.exp2(gsum[hi] - ref)]
    return jnp.concatenate(parts, axis=0)


def _scan_operands(ci, q_ref, k_ref, lg3_ref, u_ref, x_ref, dec_ref, diag_ref, rev):
    c = SCAN_CHUNK
    query_rows = _query_rows(rev)
    lane = lax.broadcasted_iota(jnp.int32, (c, LANES), 1)
    rows = _chunk_rows(ci)
    lg3 = jnp.concatenate([lg3_ref[0, rows, :], lg3_ref[1, rows, :], lg3_ref[2, rows, :]], axis=0)
    seg = _dot(u_ref[...], lg3)
    gsum = seg[0:c]
    q = q_ref[0, rows, :]
    k = k_ref[rows, :]
    block = 1
    for i in range(SCAN_LEVELS):
        if _level_half(i) >= SUBLANES:
            x = _slab_level(gsum, q, k, _level_half(i), rev)
        else:
            x = jnp.where(query_rows[i], q, k) * jnp.exp2(seg[block * c:(block + 1) * c])
            block += 1
        x_ref[i, rows, :] = x.astype(BF16)
    g_end = gsum[0:1] if rev else gsum[c - 1:c]
    x_ref[SCAN_LEVELS, rows, :] = (q * jnp.exp2(gsum)).astype(BF16)
    x_ref[SCAN_LEVELS + 1, rows, :] = (k * jnp.exp2(g_end - gsum)).astype(BF16)
    dec_ref[ci] = jnp.exp2(g_end)
    qk = q * k
    diag = jnp.zeros((c, LANES), F32)
    for hd in range(HG_HEADS):
        head_sum = jnp.sum(qk[:, hd * HG_DK:(hd + 1) * HG_DK], axis=-1, keepdims=True)
        diag = jnp.where(lane == hd, head_sum, diag)
    diag_ref[rows, :] = diag


def _chunk_rows(ci):
    return slice(ci * SCAN_CHUNK, (ci + 1) * SCAN_CHUNK)


def _scan_block(n_chunks, rev, operands, step):
    order = list(range(n_chunks))[::-1] if rev else list(range(n_chunks))
    operands(order[0])
    for i, ci in enumerate(order):
        if i + 1 < n_chunks:
            operands(order[i + 1])
        step(ci)


def _chunk_scores(x_ref, diag_ref, rows, owner):
    diag = diag_ref[rows, :]
    scores = []
    for hd in range(HG_HEADS):
        cols = slice(hd * HG_DK, (hd + 1) * HG_DK)
        s = jnp.where(owner == SCAN_LEVELS, diag[:, hd:hd + 1], 0.0)
        for i in range(SCAN_LEVELS):
            x = x_ref[i, rows, cols]
            s = jnp.where(owner == i, lax.dot_general(x, x, _NT, preferred_element_type=F32), s)
        scores.append(s.astype(BF16))
    return scores


def _scan_step(x_ref, dec, rows, cols, scores, vb, st):
    o = _dot(scores, vb) + _dot(x_ref[SCAN_LEVELS, rows, cols], st.astype(BF16))
    dec_rows = jnp.broadcast_to(dec, (HG_DK, HG_DK)).T
    st_new = st * dec_rows + lax.dot_general(x_ref[SCAN_LEVELS + 1, rows, cols], vb, _TN,
                                             preferred_element_type=F32)
    return o, st_new


def _hg_fwd_kernel(h_ref, sh_ref, sc_ref, wq_ref, wi_ref, wf_ref, lb_ref, u_ref, s0_ref,
                   o_ref, q_ref, v_ref, st_ref, k_scr, lg3_scr, x_scr, dec_scr, diag_scr, *, lb_idx):
    @pl.when(pl.program_id(1) == 0)
    def _():
        st_ref[...] = s0_ref[...]

    h = h_ref[0]
    tb = h.shape[0]
    hm = _modulated_bf16(h, sh_ref, sc_ref)
    lb = _lower_bound(lb_ref, lb_idx)
    for sl in _column_tiles(h.shape[1]):
        _forget_gate(_dot(hm, wf_ref[:, sl]), lb[:, sl], k_scr, lg3_scr, sl)
        q_ref[0, :, sl] = _silu(_dot(hm, wq_ref[:, sl]))
        v_ref[0, :, sl] = _dot(hm, wi_ref[:, sl]).astype(BF16)
    owner = _pair_owner(False)

    def operands(ci):
        _scan_operands(ci, q_ref, k_scr, lg3_scr, u_ref, x_scr, dec_scr, diag_scr, False)

    def step(ci):
        rows = _chunk_rows(ci)
        dec = dec_scr[ci]
        scores = _chunk_scores(x_scr, diag_scr, rows, owner)
        for hd in range(HG_HEADS):
            cols = slice(hd * HG_DK, (hd + 1) * HG_DK)
            o, st = _scan_step(x_scr, dec[:, cols], rows, cols, scores[hd],
                               v_ref[0, rows, cols], st_ref[0, hd])
            o_ref[0, rows, cols] = o
            st_ref[0, hd] = st

    _scan_block(tb // SCAN_CHUNK, False, operands, step)


def _hg_bwd_kernel(h_ref, sh_ref, sc_ref, ga_ref, q_ref, v_ref, of_ref, wg_ref, wf_ref, lb_ref,
                   nw_ref, wo_ref, lng_ref, lnb_ref, u_ref, s0_ref,
                   hn_ref, st_ref, k_scr, lg3_scr, x_scr, dec_scr, diag_scr, g_scr, y_scr, *, lb_idx):
    @pl.when(pl.program_id(1) == 0)
    def _():
        st_ref[...] = s0_ref[...]

    h = h_ref[0]
    tb = h.shape[0]
    hm = _modulated_bf16(h, sh_ref, sc_ref)
    lb = _lower_bound(lb_ref, lb_idx)
    for sl in _column_tiles(h.shape[1]):
        _forget_gate(_dot(hm, wf_ref[:, sl]), lb[:, sl], k_scr, lg3_scr, sl)
        g_scr[:, sl] = _silu(_dot(hm, wg_ref[:, sl]))
    owner = _pair_owner(True)
    norm_w = nw_ref[...]

    def operands(ci):
        _scan_operands(ci, q_ref, k_scr, lg3_scr, u_ref, x_scr, dec_scr, diag_scr, True)

    def step(ci):
        rows = _chunk_rows(ci)
        dec = dec_scr[ci]
        scores = _chunk_scores(x_scr, diag_scr, rows, owner)
        for hd in range(HG_HEADS):
            cols = slice(hd * HG_DK, (hd + 1) * HG_DK)
            o, st = _scan_step(x_scr, dec[:, cols], rows, cols, scores[hd],
                               v_ref[0, rows, cols], st_ref[0, hd])
            st_ref[0, hd] = st
            o = o + of_ref[0, rows, cols]
            o = o * lax.rsqrt(jnp.mean(o * o, axis=-1, keepdims=True) + RMS_EPS) * norm_w
            y_scr[rows, cols] = (o * g_scr[rows, cols]).astype(BF16)

    _scan_block(tb // SCAN_CHUNK, True, operands, step)
    y = _dot(y_scr[...], wo_ref[...])
    hn_ref[0] = _layer_norm(ALPHA * h + ga_ref[0] * y, lng_ref[...], lnb_ref[...])


def _hgrn_layer(h, mods, w, s0_f, s0_b, tb):
    b, l, d = h.shape
    nb = l // tb
    vec = _per_batch((1, d))
    state = _per_batch((HG_HEADS, HG_DK, HG_DK))
    state_shape = jax.ShapeDtypeStruct((b, HG_HEADS, HG_DK, HG_DK), F32)
    tok_shape = jax.ShapeDtypeStruct(h.shape, F32)
    n_lb = w["lb_f"].shape[0]
    seg = _resident((SCAN_SEGMENTS * SCAN_CHUNK, 3 * SCAN_CHUNK))
    scan_scratch = [pltpu.VMEM((tb, d), F32),
                    pltpu.VMEM((3, tb, d), BF16),
                    pltpu.VMEM((SCAN_OPERANDS, tb, d), BF16),
                    pltpu.VMEM((tb // SCAN_CHUNK, 1, d), F32),
                    pltpu.VMEM((tb, LANES), F32)]

    tok = pl.BlockSpec((1, tb, d), lambda i, j: (i, j, 0))
    o_f, q, v, s_f = pl.pallas_call(
        functools.partial(_hg_fwd_kernel, lb_idx=w["lb_idx"]),
        grid=(b, nb),
        in_specs=[tok, vec, vec, _resident((d, d)), _resident((d, d)), _resident((d, d)),
                  _resident((n_lb, d)), seg, state],
        out_specs=[tok, tok, tok, state],
        out_shape=[tok_shape, tok_shape, jax.ShapeDtypeStruct(h.shape, BF16), state_shape],
        scratch_shapes=scan_scratch,
        compiler_params=_params("arbitrary", "arbitrary"),
        name="hgrn2_forward_scan",
    )(h, mods["sh1"], mods["sc1"], w["wq"], w["wi"], w["wff"], w["lb_f"], _segment_matrix(False),
      s0_f)

    rtok = pl.BlockSpec((1, tb, d), lambda i, j: (i, nb - 1 - j, 0))
    h_new, s_b = pl.pallas_call(
        functools.partial(_hg_bwd_kernel, lb_idx=w["lb_idx"]),
        grid=(b, nb),
        in_specs=[rtok, vec, vec, vec, rtok, rtok, rtok,
                  _resident((d, d)), _resident((d, d)), _resident((n_lb, d)),
                  _resident((1, HG_DK)), _resident((d, d)), _resident((1, d)), _resident((1, d)),
                  seg, state],
        out_specs=[rtok, state],
        out_shape=[tok_shape, state_shape],
        scratch_shapes=scan_scratch + [pltpu.VMEM((tb, d), F32), pltpu.VMEM((tb, d), BF16)],
        compiler_params=_params("arbitrary", "arbitrary"),
        name="hgrn2_backward_scan_readout",
    )(h, mods["sh1"], mods["sc1"], mods["ga1"], q, v, o_f, w["wg"], w["wfb"], w["lb_b"],
      w["norm_w"], w["w_out"], w["ln_g"], w["ln_b"], _segment_matrix(True), s0_b)
    return h_new, s_f, s_b


def _token_block(l):
    for tb in (512, 256, 128):
        if l % tb == 0:
            return tb
    raise ValueError(f"sequence length {l} is not a multiple of 128")


def kernel(x, c, ctx, c_ctx, ada_w, ada_b, ln_g, ln_b, hg_w_in, hg_lb, hg_norm_w, hg_w_out, sgu_w_in, sgu_ln_g, sgu_ln_b, sgu_w_s, sgu_b_s, sgu_w_out, ffn_w_up, ffn_conv_w, ffn_conv_b, ffn_w_down):
    batch, seq, d = x.shape
    ctx_len = ctx.shape[1]
    assert d == D_MODEL and batch + 1 <= ADA_ROWS
    tb_x, tb_c = _token_block(seq), _token_block(ctx_len)
    assert tb_x % GRID_W == 0 and tb_c == ctx_len

    cc = jnp.concatenate([c, c_ctx[None], jnp.zeros((ADA_ROWS - batch - 1, d), F32)], axis=0)
    ada = _ada_modulation(cc, ada_w, ada_b)
    names = ("sh1", "sc1", "ga1", "sh2", "sc2", "ga2")

    def mods_for(layer, is_ctx):
        m = ada[layer].reshape(ADA_ROWS, 6, 1, d)
        if is_ctx:
            rows = jnp.broadcast_to(m[batch:batch + 1], (batch, 6, 1, d))
        else:
            rows = m[:batch]
        return {n: rows[:, i] for i, n in enumerate(names)}

    k1, k2 = HG_HEADS * HG_DK, 2 * HG_HEADS * HG_DK
    zero_state = jnp.zeros((batch, HG_HEADS, HG_DK, HG_DK), F32)
    h, hc = x, ctx
    for layer in range(DEPTH):
        kind, idx = layer % N_MIXERS, layer // N_MIXERS
        ctx_later = any(j % N_MIXERS == 0 for j in range(layer + 1, DEPTH))
        mx = mods_for(layer, False)
        mc = mods_for(layer, True) if (kind == 0 or ctx_later) else None
        ln1 = {"ln_g": ln_g[layer, 0][None], "ln_b": ln_b[layer, 0][None]}
        if kind == 0:
            wi = hg_w_in[idx].astype(BF16)
            w = dict(ln1, wq=wi[:, :k1], wg=wi[:, k1:k2], wi=wi[:, k2:k2 + k1],
                     wff=wi[:, k2 + k1:k2 + 2 * k1], wfb=wi[:, k2 + 2 * k1:],
                     lb_f=hg_lb[0], lb_b=hg_lb[1], lb_idx=idx,
                     norm_w=hg_norm_w[idx][None], w_out=hg_w_out[idx].astype(BF16))
            hc_new, s_f, s_b = _hgrn_layer(hc, mc, w, zero_state, zero_state, tb_c)
            h, _, _ = _hgrn_layer(h, mx, w, s_f, s_b, tb_x)
            if ctx_later:
                hc = hc_new
        else:
            w = dict(ln1, w_in=sgu_w_in[idx].astype(BF16), v_g=sgu_ln_g[idx][None],
                     v_b=sgu_ln_b[idx][None], w_s=sgu_w_s[idx].astype(BF16),
                     b_s=jnp.repeat(sgu_b_s[idx].T, SGU_GW, axis=1),
                     w_out=sgu_w_out[idx].astype(BF16))
            h = _sgu_layer(h, mx, w, tb_x)
            if ctx_later:
                hc = _sgu_layer(hc, mc, w, tb_c)
        up = ffn_w_up[layer].astype(BF16)
        w = dict(wa=up[:, :D_FF], wg=up[:, D_FF:], cwa=ffn_conv_w[layer][:, :D_FF],
                 cwg=ffn_conv_w[layer][:, D_FF:], cba=ffn_conv_b[layer][None, :D_FF],
                 cbg=ffn_conv_b[layer][None, D_FF:], wd=ffn_w_down[layer].astype(BF16),
                 ln_g=ln_g[layer, 1][None], ln_b=ln_b[layer, 1][None])
        h = _ffn_layer(h, mx, w, GRID_W, tb_x)
        if ctx_later:
            hc = _ffn_layer(hc, mc, w, ctx_len, tb_c)
    return h
```

```python
import functools

import jax
import jax.numpy as jnp
import numpy as np
from jax import lax
from jax.experimental import pallas as pl
from jax.experimental.pallas import tpu as pltpu

F32 = jnp.float32
BF16 = jnp.bfloat16

LANES = 128
D_MODEL = 1024
DEPTH = 4
N_MIXERS = 2
GRID_W = 64
HG_DK = LANES
HG_HEADS = D_MODEL // HG_DK
SCAN_CHUNK = 64
SCAN_LEVELS = 6
SUBLANES = 8
SCAN_SEGMENTS = 4
SCAN_OPERANDS = SCAN_LEVELS + 2
LOG2_E = 1.4426950408889634
SGU_CHUNK = 128
SGU_WIDTH = 3 * D_MODEL
SGU_GROUPS = 8
SGU_GW = SGU_WIDTH // SGU_GROUPS
SGU_TILE = 2 * SGU_GW
D_FF = 128 * ((8 * D_MODEL // 3 + 127) // 128)
FF_TILE = 256
PROJ_TILE = 256
PROJ_ROWS = 256
ALPHA = (2.0 * DEPTH) ** 0.25
LN_EPS = 1e-5
RMS_EPS = 1e-6
VMEM_LIMIT_BYTES = 56 * 1024 * 1024
ADA_TILE = 1536
ADA_ROWS = 8

_NT = (((1,), (1,)), ((), ()))
_TN = (((0,), (0,)), ((), ()))


def _params(*sem):
    return pltpu.CompilerParams(dimension_semantics=sem, vmem_limit_bytes=VMEM_LIMIT_BYTES)


def _resident(shape):
    zeros = (0,) * len(shape)
    return pl.BlockSpec(shape, lambda *_: zeros, pipeline_mode=pl.Buffered(1))


def _per_batch(shape):
    zeros = (0,) * len(shape)
    return pl.BlockSpec((1,) + tuple(shape), lambda b, j: (b,) + zeros)


def _dot(a, b):
    return jnp.dot(a, b, preferred_element_type=F32)


def _silu(x):
    return x * jax.nn.sigmoid(x)


def _gelu(x):
    return 0.5 * x * (1.0 + lax.erf(x * (2.0 ** -0.5)))


def _layer_norm(x, g, b):
    mu = jnp.mean(x, axis=-1, keepdims=True)
    xc = x - mu
    var = jnp.mean(xc * xc, axis=-1, keepdims=True)
    return xc * lax.rsqrt(var + LN_EPS) * g + b


def _modulated_bf16(h, sh_ref, sc_ref):
    return (h * (1.0 + sc_ref[0]) + sh_ref[0]).astype(BF16)


def _ada_kernel(c_ref, w_ref, b_ref, o_ref):
    s = _silu(c_ref[...])
    o_ref[0] = jnp.dot(s, w_ref[0], precision=lax.Precision.HIGHEST,
                       preferred_element_type=F32) + b_ref[0]


def _ada_modulation(cc, ada_w, ada_b):
    n = ada_w.shape[-1]
    return pl.pallas_call(
        _ada_kernel,
        grid=(DEPTH, n // ADA_TILE),
        in_specs=[pl.BlockSpec((ADA_ROWS, D_MODEL), lambda l, j: (0, 0)),
                  pl.BlockSpec((1, D_MODEL, ADA_TILE), lambda l, j: (l, 0, j)),
                  pl.BlockSpec((1, 1, ADA_TILE), lambda l, j: (l, 0, j))],
        out_specs=pl.BlockSpec((1, ADA_ROWS, ADA_TILE), lambda l, j: (l, 0, j)),
        out_shape=jax.ShapeDtypeStruct((DEPTH, ADA_ROWS, n), F32),
        compiler_params=_params("arbitrary", "arbitrary"),
        name="ada_modulation",
    )(cc, ada_w, ada_b.reshape(DEPTH, 1, n))


def _ffn_kernel(h_ref, sh_ref, sc_ref, ga_ref, wa_ref, wg_ref, cwa_ref, cwg_ref, cba_ref, cbg_ref,
                wd_ref, lng_ref, lnb_ref, o_ref, act_ref, *, row_len):
    h = h_ref[0]
    tb = h.shape[0]
    hm = _modulated_bf16(h, sh_ref, sc_ref)
    col = lax.broadcasted_iota(jnp.int32, (tb, FF_TILE), 0) % row_len
    has_prev = col != 0
    has_next = col != row_len - 1

    def conv(y, cw_ref, cb_ref, sl):
        y_prev = jnp.where(has_prev, pltpu.roll(y, 1, 0), 0.0)
        y_next = jnp.where(has_next, pltpu.roll(y, tb - 1, 0), 0.0)
        return (y_prev * cw_ref[0:1, sl] + y * cw_ref[1:2, sl] + y_next * cw_ref[2:3, sl]
                + cb_ref[:, sl])

    for j in range(D_FF // FF_TILE):
        sl = slice(j * FF_TILE, (j + 1) * FF_TILE)
        a = conv(_dot(hm, wa_ref[:, sl]), cwa_ref, cba_ref, sl)
        g = conv(_dot(hm, wg_ref[:, sl]), cwg_ref, cbg_ref, sl)
        act_ref[:, sl] = (_silu(g) * a).astype(BF16)
    f = _dot(act_ref[...], wd_ref[...])
    o_ref[0] = _layer_norm(ALPHA * h + ga_ref[0] * f, lng_ref[...], lnb_ref[...])


def _ffn_layer(h, mods, w, row_len, tb):
    b, l, d = h.shape
    tok = pl.BlockSpec((1, tb, d), lambda i, j: (i, j, 0))
    vec = _per_batch((1, d))
    return pl.pallas_call(
        functools.partial(_ffn_kernel, row_len=row_len),
        grid=(b, l // tb),
        in_specs=[tok, vec, vec, vec,
                  _resident((d, D_FF)), _resident((d, D_FF)),
                  _resident((3, D_FF)), _resident((3, D_FF)),
                  _resident((1, D_FF)), _resident((1, D_FF)),
                  _resident((D_FF, d)), _resident((1, d)), _resident((1, d))],
        out_specs=tok,
        out_shape=jax.ShapeDtypeStruct(h.shape, F32),
        scratch_shapes=[pltpu.VMEM((tb, D_FF), BF16)],
        compiler_params=_params("arbitrary", "arbitrary"),
        name="conv_ffn",
    )(h, mods["sh2"], mods["sc2"], mods["ga2"], w["wa"], w["wg"], w["cwa"], w["cwg"],
      w["cba"], w["cbg"], w["wd"], w["ln_g"], w["ln_b"])


def _sgu_kernel(h_ref, sh_ref, sc_ref, ga_ref, win_ref, vg_ref, vb_ref, ws_ref, bs_ref, wout_ref,
                lng_ref, lnb_ref, o_ref, v_ref, p_ref):
    h = h_ref[0]
    tb = h.shape[0]
    hm = _modulated_bf16(h, sh_ref, sc_ref)
    tiles = [slice(j, j + SGU_TILE) for j in range(0, SGU_WIDTH, SGU_TILE)]

    total = jnp.zeros((tb, 1), F32)
    for sl in tiles:
        z = _gelu(_dot(hm, win_ref[:, SGU_WIDTH + sl.start:SGU_WIDTH + sl.stop]))
        v_ref[:, sl] = z
        total = total + jnp.sum(z, axis=-1, keepdims=True)
    mu = total * (1.0 / SGU_WIDTH)
    sq = jnp.zeros((tb, 1), F32)
    for sl in tiles:
        zc = v_ref[:, sl] - mu
        sq = sq + jnp.sum(zc * zc, axis=-1, keepdims=True)
    rstd = lax.rsqrt(sq * (1.0 / SGU_WIDTH) + LN_EPS)

    for t, sl in enumerate(tiles):
        vn = ((v_ref[:, sl] - mu) * rstd * vg_ref[:, sl] + vb_ref[:, sl]).astype(BF16)
        u = _gelu(_dot(hm, win_ref[:, sl]))
        for gi in range(SGU_TILE // SGU_GW):
            g = t * (SGU_TILE // SGU_GW) + gi
            lo = slice(gi * SGU_GW, (gi + 1) * SGU_GW)
            gl = slice(g * SGU_GW, (g + 1) * SGU_GW)
            for n in range(tb // SGU_CHUNK):
                rows = slice(n * SGU_CHUNK, (n + 1) * SGU_CHUNK)
                mixed = _dot(ws_ref[g], vn[rows, lo]) + bs_ref[:, gl]
                p_ref[rows, gl] = (u[rows, lo] * mixed).astype(BF16)
    y = _dot(p_ref[...], wout_ref[...])
    o_ref[0] = _layer_norm(ALPHA * h + ga_ref[0] * y, lng_ref[...], lnb_ref[...])


def _sgu_layer(h, mods, w, tb):
    b, l, d = h.shape
    tok = pl.BlockSpec((1, tb, d), lambda i, j: (i, j, 0))
    vec = _per_batch((1, d))
    return pl.pallas_call(
        _sgu_kernel,
        grid=(b, l // tb),
        in_specs=[tok, vec, vec, vec,
                  _resident((d, 2 * SGU_WIDTH)), _resident((1, SGU_WIDTH)), _resident((1, SGU_WIDTH)),
                  _resident((SGU_GROUPS, SGU_CHUNK, SGU_CHUNK)), _resident((SGU_CHUNK, SGU_WIDTH)),
                  _resident((SGU_WIDTH, d)), _resident((1, d)), _resident((1, d))],
        out_specs=tok,
        out_shape=jax.ShapeDtypeStruct(h.shape, F32),
        scratch_shapes=[pltpu.VMEM((tb, SGU_WIDTH), F32), pltpu.VMEM((tb, SGU_WIDTH), BF16)],
        compiler_params=_params("arbitrary", "arbitrary"),
        name="chunk_sgu",
    )(h, mods["sh1"], mods["sc1"], mods["ga1"], w["w_in"], w["v_g"], w["v_b"], w["w_s"], w["b_s"],
      w["w_out"], w["ln_g"], w["ln_b"])


def _lower_bound(lb_ref, idx):
    x = lb_ref[...]
    e = jnp.exp(x - jnp.max(x, axis=0, keepdims=True))
    p = e / jnp.sum(e, axis=0, keepdims=True)
    acc = p[0:1]
    for r in range(1, idx + 1):
        acc = acc + p[r:r + 1]
    return acc - p[0:1]


def _level_half(i):
    return (SCAN_CHUNK // 2) >> i


def _segment_matrix(rev):
    c = SCAN_CHUNK
    t = np.arange(c)[:, None]
    r = np.arange(c)[None, :]
    blocks = [r >= t] if rev else [r <= t]
    for i in range(SCAN_LEVELS):
        m = _level_half(i)
        if m >= SUBLANES:
            continue
        upper = (t % (2 * m)) >= m
        if rev:
            ref = t - t % (2 * m) + m
            blocks.append(np.where(upper, (r >= ref) & (r < t), (r >= t) & (r < ref)))
        else:
            ref = t - t % (2 * m) + m - 1
            blocks.append(np.where(upper, (r > ref) & (r <= t), (r > t) & (r <= ref)))
    assert len(blocks) == SCAN_SEGMENTS
    u = np.concatenate(blocks, axis=0).astype(np.float32)
    return jnp.asarray(np.tile(u, (1, 3)), BF16)


def _pair_owner(rev):
    c = SCAN_CHUNK
    t = lax.broadcasted_iota(jnp.int32, (c, c), 0)
    s = lax.broadcasted_iota(jnp.int32, (c, c), 1)
    x = t ^ s
    owner = jnp.where(t == s, SCAN_LEVELS, -1)
    for i in range(SCAN_LEVELS):
        m = (c // 2) >> i
        owner = jnp.where((x >= m) & (x < 2 * m), i, owner)
    return jnp.where((t <= s) if rev else (t >= s), owner, -1)


def _query_rows(rev):
    row = lax.broadcasted_iota(jnp.int32, (SCAN_CHUNK, D_MODEL), 0)
    masks = []
    for i in range(SCAN_LEVELS):
        upper = (row & ((SCAN_CHUNK // 2) >> i)) != 0
        masks.append(jnp.logical_not(upper) if rev else upper)
    return masks


def _column_tiles(width):
    return [slice(j, j + PROJ_TILE) for j in range(0, width, PROJ_TILE)]


def _forget_gate(z, lb, k_ref, lg3_ref, rows, sl):
    f = lb + (1.0 - lb) * jax.nn.sigmoid(z)
    k_ref[rows, sl] = 1.0 - f
    lg = jnp.log(f) * LOG2_E
    hi = lg.astype(BF16)
    rest = lg - hi.astype(F32)
    mid = rest.astype(BF16)
    lg3_ref[0, rows, sl] = hi
    lg3_ref[1, rows, sl] = mid
    lg3_ref[2, rows, sl] = (rest - mid.astype(F32)).astype(BF16)


def _slab_level(gsum, q, k, m, rev):
    parts = []
    for start in range(0, SCAN_CHUNK, 2 * m):
        lo, hi = slice(start, start + m), slice(start + m, start + 2 * m)
        if rev:
            ref = gsum[start + m:start + m + 1]
            parts += [q[lo] * jnp.exp2(gsum[lo] - ref), k[hi] * jnp.exp2(ref - gsum[hi])]
        else:
            ref = gsum[start + m - 1:start + m]
            parts += [k[lo] * jnp.exp2(ref - gsum[lo]), q[hi] * jnp.exp2(gsum[hi] - ref)]
    return jnp.concatenate(parts, axis=0)


def _scan_operands(ci, q_ref, k_ref, lg3_ref, u_ref, x_ref, dec_ref, diag_ref, rev):
    c = SCAN_CHUNK
    query_rows = _query_rows(rev)
    lane = lax.broadcasted_iota(jnp.int32, (c, LANES), 1)
    rows = _chunk_rows(ci)
    lg3 = jnp.concatenate([lg3_ref[0, rows, :], lg3_ref[1, rows, :], lg3_ref[2, rows, :]], axis=0)
    seg = _dot(u_ref[...], lg3)
    gsum = seg[0:c]
    q = q_ref[0, rows, :]
    k = k_ref[rows, :]
    block = 1
    for i in range(SCAN_LEVELS):
        if _level_half(i) >= SUBLANES:
            x = _slab_level(gsum, q, k, _level_half(i), rev)
        else:
            x = jnp.where(query_rows[i], q, k) * jnp.exp2(seg[block * c:(block + 1) * c])
            block += 1
        x_ref[i, rows, :] = x.astype(BF16)
    g_end = gsum[0:1] if rev else gsum[c - 1:c]
    x_ref[SCAN_LEVELS, rows, :] = (q * jnp.exp2(gsum)).astype(BF16)
    x_ref[SCAN_LEVELS + 1, rows, :] = (k * jnp.exp2(g_end - gsum)).astype(BF16)
    dec_ref[ci] = jnp.exp2(g_end)
    qk = q * k
    diag = jnp.zeros((c, LANES), F32)
    for hd in range(HG_HEADS):
        head_sum = jnp.sum(qk[:, hd * HG_DK:(hd + 1) * HG_DK], axis=-1, keepdims=True)
        diag = jnp.where(lane == hd, head_sum, diag)
    diag_ref[rows, :] = diag


def _chunk_rows(ci):
    return slice(ci * SCAN_CHUNK, (ci + 1) * SCAN_CHUNK)


def _scan_block(tb, rev, project, operands, step, finish=None):
    group_rows = min(tb, PROJ_ROWS)
    per_group = group_rows // SCAN_CHUNK
    starts = list(range(0, tb, group_rows))[::-1] if rev else list(range(0, tb, group_rows))
    groups = []
    for r0 in starts:
        chunks = [r0 // SCAN_CHUNK + j for j in range(per_group)]
        groups.append((slice(r0, r0 + group_rows), chunks[::-1] if rev else chunks))
    for thunk in project(groups[0][0]):
        thunk()
    operands(groups[0][1][0])
    for g, (rows, chunks) in enumerate(groups):
        fillers = list(project(groups[g + 1][0])) if g + 1 < len(groups) else []
        if finish is not None and g > 0:
            fillers.insert(0, functools.partial(finish, groups[g - 1][0]))
        slots = max(per_group - 1, 1)
        for j, ci in enumerate(chunks):
            take = -(-len(fillers) // (slots - j)) if j < slots else len(fillers)
            for thunk in fillers[:take]:
                thunk()
            fillers = fillers[take:]
            if j + 1 < per_group:
                operands(chunks[j + 1])
            elif g + 1 < len(groups):
                operands(groups[g + 1][1][0])
            step(ci)
    if finish is not None:
        finish(groups[-1][0])


def _chunk_scores(x_ref, diag_ref, rows, owner):
    diag = diag_ref[rows, :]
    scores = []
    for hd in range(HG_HEADS):
        cols = slice(hd * HG_DK, (hd + 1) * HG_DK)
        s = jnp.where(owner == SCAN_LEVELS, diag[:, hd:hd + 1], 0.0)
        for i in range(SCAN_LEVELS):
            x = x_ref[i, rows, cols]
            s = jnp.where(owner == i, lax.dot_general(x, x, _NT, preferred_element_type=F32), s)
        scores.append(s.astype(BF16))
    return scores


def _scan_step(x_ref, dec, rows, cols, scores, vb, st):
    o = _dot(scores, vb) + _dot(x_ref[SCAN_LEVELS, rows, cols], st.astype(BF16))
    dec_rows = jnp.broadcast_to(dec, (HG_DK, HG_DK)).T
    st_new = st * dec_rows + lax.dot_general(x_ref[SCAN_LEVELS + 1, rows, cols], vb, _TN,
                                             preferred_element_type=F32)
    return o, st_new


def _hg_fwd_kernel(h_ref, sh_ref, sc_ref, wq_ref, wi_ref, wf_ref, lb_ref, u_ref, s0_ref,
                   o_ref, q_ref, v_ref, st_ref, k_scr, lg3_scr, x_scr, dec_scr, diag_scr, *, lb_idx):
    @pl.when(pl.program_id(1) == 0)
    def _():
        st_ref[...] = s0_ref[...]

    tb = h_ref.shape[1]
    lb = _lower_bound(lb_ref, lb_idx)

    def project(rows):
        hm = _modulated_bf16(h_ref[0, rows, :], sh_ref, sc_ref)

        def tile(sl):
            _forget_gate(_dot(hm, wf_ref[:, sl]), lb[:, sl], k_scr, lg3_scr, rows, sl)
            q_ref[0, rows, sl] = _silu(_dot(hm, wq_ref[:, sl]))
            v_ref[0, rows, sl] = _dot(hm, wi_ref[:, sl]).astype(BF16)

        return [functools.partial(tile, sl) for sl in _column_tiles(h_ref.shape[2])]

    owner = _pair_owner(False)

    def operands(ci):
        _scan_operands(ci, q_ref, k_scr, lg3_scr, u_ref, x_scr, dec_scr, diag_scr, False)

    def step(ci):
        rows = _chunk_rows(ci)
        dec = dec_scr[ci]
        scores = _chunk_scores(x_scr, diag_scr, rows, owner)
        for hd in range(HG_HEADS):
            cols = slice(hd * HG_DK, (hd + 1) * HG_DK)
            o, st = _scan_step(x_scr, dec[:, cols], rows, cols, scores[hd],
                               v_ref[0, rows, cols], st_ref[0, hd])
            o_ref[0, rows, cols] = o
            st_ref[0, hd] = st

    _scan_block(tb, False, project, operands, step)


def _hg_bwd_kernel(h_ref, sh_ref, sc_ref, ga_ref, q_ref, v_ref, of_ref, wg_ref, wf_ref, lb_ref,
                   nw_ref, wo_ref, lng_ref, lnb_ref, u_ref, s0_ref,
                   hn_ref, st_ref, k_scr, lg3_scr, x_scr, dec_scr, diag_scr, g_scr, y_scr, *, lb_idx):
    @pl.when(pl.program_id(1) == 0)
    def _():
        st_ref[...] = s0_ref[...]

    tb = h_ref.shape[1]
    lb = _lower_bound(lb_ref, lb_idx)

    def project(rows):
        hm = _modulated_bf16(h_ref[0, rows, :], sh_ref, sc_ref)

        def tile(sl):
            _forget_gate(_dot(hm, wf_ref[:, sl]), lb[:, sl], k_scr, lg3_scr, rows, sl)
            g_scr[rows, sl] = _silu(_dot(hm, wg_ref[:, sl]))

        return [functools.partial(tile, sl) for sl in _column_tiles(h_ref.shape[2])]

    def finish(rows):
        y = _dot(y_scr[rows, :], wo_ref[...])
        hn_ref[0, rows, :] = _layer_norm(ALPHA * h_ref[0, rows, :] + ga_ref[0] * y,
                                         lng_ref[...], lnb_ref[...])

    owner = _pair_owner(True)
    norm_w = nw_ref[...]

    def operands(ci):
        _scan_operands(ci, q_ref, k_scr, lg3_scr, u_ref, x_scr, dec_scr, diag_scr, True)

    def step(ci):
        rows = _chunk_rows(ci)
        dec = dec_scr[ci]
        scores = _chunk_scores(x_scr, diag_scr, rows, owner)
        for hd in range(HG_HEADS):
            cols = slice(hd * HG_DK, (hd + 1) * HG_DK)
            o, st = _scan_step(x_scr, dec[:, cols], rows, cols, scores[hd],
                               v_ref[0, rows, cols], st_ref[0, hd])
            st_ref[0, hd] = st
            o = o + of_ref[0, rows, cols]
            o = o * lax.rsqrt(jnp.mean(o * o, axis=-1, keepdims=True) + RMS_EPS) * norm_w
            y_scr[rows, cols] = (o * g_scr[rows, cols]).astype(BF16)

    _scan_block(tb, True, project, operands, step, finish)


def _hgrn_layer(h, mods, w, s0_f, s0_b, tb):
    b, l, d = h.shape
    nb = l // tb
    vec = _per_batch((1, d))
    state = _per_batch((HG_HEADS, HG_DK, HG_DK))
    state_shape = jax.ShapeDtypeStruct((b, HG_HEADS, HG_DK, HG_DK), F32)
    tok_shape = jax.ShapeDtypeStruct(h.shape, F32)
    n_lb = w["lb_f"].shape[0]
    seg = _resident((SCAN_SEGMENTS * SCAN_CHUNK, 3 * SCAN_CHUNK))
    scan_scratch = [pltpu.VMEM((tb, d), F32),
                    pltpu.VMEM((3, tb, d), BF16),
                    pltpu.VMEM((SCAN_OPERANDS, tb, d), BF16),
                    pltpu.VMEM((tb // SCAN_CHUNK, 1, d), F32),
                    pltpu.VMEM((tb, LANES), F32)]

    tok = pl.BlockSpec((1, tb, d), lambda i, j: (i, j, 0))
    o_f, q, v, s_f = pl.pallas_call(
        functools.partial(_hg_fwd_kernel, lb_idx=w["lb_idx"]),
        grid=(b, nb),
        in_specs=[tok, vec, vec, _resident((d, d)), _resident((d, d)), _resident((d, d)),
                  _resident((n_lb, d)), seg, state],
        out_specs=[tok, tok, tok, state],
        out_shape=[tok_shape, tok_shape, jax.ShapeDtypeStruct(h.shape, BF16), state_shape],
        scratch_shapes=scan_scratch,
        compiler_params=_params("arbitrary", "arbitrary"),
        name="hgrn2_forward_scan",
    )(h, mods["sh1"], mods["sc1"], w["wq"], w["wi"], w["wff"], w["lb_f"], _segment_matrix(False),
      s0_f)

    rtok = pl.BlockSpec((1, tb, d), lambda i, j: (i, nb - 1 - j, 0))
    h_new, s_b = pl.pallas_call(
        functools.partial(_hg_bwd_kernel, lb_idx=w["lb_idx"]),
        grid=(b, nb),
        in_specs=[rtok, vec, vec, vec, rtok, rtok, rtok,
                  _resident((d, d)), _resident((d, d)), _resident((n_lb, d)),
                  _resident((1, HG_DK)), _resident((d, d)), _resident((1, d)), _resident((1, d)),
                  seg, state],
        out_specs=[rtok, state],
        out_shape=[tok_shape, state_shape],
        scratch_shapes=scan_scratch + [pltpu.VMEM((tb, d), F32), pltpu.VMEM((tb, d), BF16)],
        compiler_params=_params("arbitrary", "arbitrary"),
        name="hgrn2_backward_scan_readout",
    )(h, mods["sh1"], mods["sc1"], mods["ga1"], q, v, o_f, w["wg"], w["wfb"], w["lb_b"],
      w["norm_w"], w["w_out"], w["ln_g"], w["ln_b"], _segment_matrix(True), s0_b)
    return h_new, s_f, s_b


def _token_block(l):
    for tb in (512, 256, 128):
        if l % tb == 0:
            return tb
    raise ValueError(f"sequence length {l} is not a multiple of 128")


def kernel(x, c, ctx, c_ctx, ada_w, ada_b, ln_g, ln_b, hg_w_in, hg_lb, hg_norm_w, hg_w_out, sgu_w_in, sgu_ln_g, sgu_ln_b, sgu_w_s, sgu_b_s, sgu_w_out, ffn_w_up, ffn_conv_w, ffn_conv_b, ffn_w_down):
    batch, seq, d = x.shape
    ctx_len = ctx.shape[1]
    assert d == D_MODEL and batch + 1 <= ADA_ROWS
    tb_x, tb_c = _token_block(seq), _token_block(ctx_len)
    assert tb_x % GRID_W == 0 and tb_c == ctx_len

    cc = jnp.concatenate([c, c_ctx[None], jnp.zeros((ADA_ROWS - batch - 1, d), F32)], axis=0)
    ada = _ada_modulation(cc, ada_w, ada_b)
    names = ("sh1", "sc1", "ga1", "sh2", "sc2", "ga2")

    def mods_for(layer, is_ctx):
        m = ada[layer].reshape(ADA_ROWS, 6, 1, d)
        if is_ctx:
            rows = jnp.broadcast_to(m[batch:batch + 1], (batch, 6, 1, d))
        else:
            rows = m[:batch]
        return {n: rows[:, i] for i, n in enumerate(names)}

    k1, k2 = HG_HEADS * HG_DK, 2 * HG_HEADS * HG_DK
    zero_state = jnp.zeros((batch, HG_HEADS, HG_DK, HG_DK), F32)
    h, hc = x, ctx
    for layer in range(DEPTH):
        kind, idx = layer % N_MIXERS, layer // N_MIXERS
        ctx_later = any(j % N_MIXERS == 0 for j in range(layer + 1, DEPTH))
        mx = mods_for(layer, False)
        mc = mods_for(layer, True) if (kind == 0 or ctx_later) else None
        ln1 = {"ln_g": ln_g[layer, 0][None], "ln_b": ln_b[layer, 0][None]}
        if kind == 0:
            wi = hg_w_in[idx].astype(BF16)
            w = dict(ln1, wq=wi[:, :k1], wg=wi[:, k1:k2], wi=wi[:, k2:k2 + k1],
                     wff=wi[:, k2 + k1:k2 + 2 * k1], wfb=wi[:, k2 + 2 * k1:],
                     lb_f=hg_lb[0], lb_b=hg_lb[1], lb_idx=idx,
                     norm_w=hg_norm_w[idx][None], w_out=hg_w_out[idx].astype(BF16))
            hc_new, s_f, s_b = _hgrn_layer(hc, mc, w, zero_state, zero_state, tb_c)
            h, _, _ = _hgrn_layer(h, mx, w, s_f, s_b, tb_x)
            if ctx_later:
                hc = hc_new
        else:
            w = dict(ln1, w_in=sgu_w_in[idx].astype(BF16), v_g=sgu_ln_g[idx][None],
                     v_b=sgu_ln_b[idx][None], w_s=sgu_w_s[idx].astype(BF16),
                     b_s=jnp.repeat(sgu_b_s[idx].T, SGU_GW, axis=1),
                     w_out=sgu_w_out[idx].astype(BF16))
            h = _sgu_layer(h, mx, w, tb_x)
            if ctx_later:
                hc = _sgu_layer(hc, mc, w, tb_c)
        up = ffn_w_up[layer].astype(BF16)
        w = dict(wa=up[:, :D_FF], wg=up[:, D_FF:], cwa=ffn_conv_w[layer][:, :D_FF],
                 cwg=ffn_conv_w[layer][:, D_FF:], cba=ffn_conv_b[layer][None, :D_FF],
                 cbg=ffn_conv_b[layer][None, D_FF:], wd=ffn_w_down[layer].astype(BF16),
                 ln_g=ln_g[layer, 1][None], ln_b=ln_b[layer, 1][None])
        h = _ffn_layer(h, mx, w, GRID_W, tb_x)
        if ctx_later:
            hc = _ffn_layer(hc, mc, w, ctx_len, tb_c)
    return h
```

```python
import functools

import jax
import jax.numpy as jnp
import numpy as np
from jax import lax
from jax.experimental import pallas as pl
from jax.experimental.pallas import tpu as pltpu

F32 = jnp.float32
BF16 = jnp.bfloat16

LANES = 128
D_MODEL = 1024
DEPTH = 4
N_MIXERS = 2
GRID_W = 64
HG_DK = LANES
HG_HEADS = D_MODEL // HG_DK
SCAN_CHUNK = 64
SCAN_LEVELS = 6
SUBLANES = 8
SCAN_SEGMENTS = 4
SCAN_OPERANDS = SCAN_LEVELS + 2
LOG2_E = 1.4426950408889634
SGU_CHUNK = 128
SGU_WIDTH = 3 * D_MODEL
SGU_GROUPS = 8
SGU_GW = SGU_WIDTH // SGU_GROUPS
SGU_TILE = 2 * SGU_GW
D_FF = 128 * ((8 * D_MODEL // 3 + 127) // 128)
FF_TILE = 256
PROJ_TILE = 256
PROJ_ROWS = 256
ALPHA = (2.0 * DEPTH) ** 0.25
LN_EPS = 1e-5
RMS_EPS = 1e-6
VMEM_LIMIT_BYTES = 56 * 1024 * 1024
ADA_TILE = 1536
ADA_ROWS = 8

_NT = (((1,), (1,)), ((), ()))
_TN = (((0,), (0,)), ((), ()))


def _params(*sem):
    return pltpu.CompilerParams(dimension_semantics=sem, vmem_limit_bytes=VMEM_LIMIT_BYTES)


def _resident(shape):
    zeros = (0,) * len(shape)
    return pl.BlockSpec(shape, lambda *_: zeros, pipeline_mode=pl.Buffered(1))


def _per_batch(shape):
    zeros = (0,) * len(shape)
    return pl.BlockSpec((1,) + tuple(shape), lambda b, j: (b,) + zeros)


def _dot(a, b):
    return jnp.dot(a, b, preferred_element_type=F32)


def _silu(x):
    return x * jax.nn.sigmoid(x)


def _gelu(x):
    return 0.5 * x * (1.0 + lax.erf(x * (2.0 ** -0.5)))


def _layer_norm(x, g, b):
    mu = jnp.mean(x, axis=-1, keepdims=True)
    xc = x - mu
    var = jnp.mean(xc * xc, axis=-1, keepdims=True)
    return xc * lax.rsqrt(var + LN_EPS) * g + b


def _modulated_bf16(h, sh_ref, sc_ref):
    return (h * (1.0 + sc_ref[0]) + sh_ref[0]).astype(BF16)


def _ada_kernel(c_ref, w_ref, b_ref, o_ref):
    s = _silu(c_ref[...])
    o_ref[0] = jnp.dot(s, w_ref[0], precision=lax.Precision.HIGHEST,
                       preferred_element_type=F32) + b_ref[0]


def _ada_modulation(cc, ada_w, ada_b):
    n = ada_w.shape[-1]
    return pl.pallas_call(
        _ada_kernel,
        grid=(DEPTH, n // ADA_TILE),
        in_specs=[pl.BlockSpec((ADA_ROWS, D_MODEL), lambda l, j: (0, 0)),
                  pl.BlockSpec((1, D_MODEL, ADA_TILE), lambda l, j: (l, 0, j)),
                  pl.BlockSpec((1, 1, ADA_TILE), lambda l, j: (l, 0, j))],
        out_specs=pl.BlockSpec((1, ADA_ROWS, ADA_TILE), lambda l, j: (l, 0, j)),
        out_shape=jax.ShapeDtypeStruct((DEPTH, ADA_ROWS, n), F32),
        compiler_params=_params("arbitrary", "arbitrary"),
        name="ada_modulation",
    )(cc, ada_w, ada_b.reshape(DEPTH, 1, n))


def _ffn_kernel(h_ref, sh_ref, sc_ref, ga_ref, wa_ref, wg_ref, cwa_ref, cwg_ref, cba_ref, cbg_ref,
                wd_ref, lng_ref, lnb_ref, o_ref, act_ref, *, row_len):
    h = h_ref[0]
    tb = h.shape[0]
    hm = _modulated_bf16(h, sh_ref, sc_ref)
    col = lax.broadcasted_iota(jnp.int32, (tb, FF_TILE), 0) % row_len
    has_prev = col != 0
    has_next = col != row_len - 1

    def conv(y, cw_ref, cb_ref, sl):
        y_prev = jnp.where(has_prev, pltpu.roll(y, 1, 0), 0.0)
        y_next = jnp.where(has_next, pltpu.roll(y, tb - 1, 0), 0.0)
        return (y_prev * cw_ref[0:1, sl] + y * cw_ref[1:2, sl] + y_next * cw_ref[2:3, sl]
                + cb_ref[:, sl])

    for j in range(D_FF // FF_TILE):
        sl = slice(j * FF_TILE, (j + 1) * FF_TILE)
        a = conv(_dot(hm, wa_ref[:, sl]), cwa_ref, cba_ref, sl)
        g = conv(_dot(hm, wg_ref[:, sl]), cwg_ref, cbg_ref, sl)
        act_ref[:, sl] = (_silu(g) * a).astype(BF16)
    f = _dot(act_ref[...], wd_ref[...])
    o_ref[0] = _layer_norm(ALPHA * h + ga_ref[0] * f, lng_ref[...], lnb_ref[...])


def _ffn_layer(h, mods, w, row_len, tb):
    b, l, d = h.shape
    tok = pl.BlockSpec((1, tb, d), lambda i, j: (i, j, 0))
    vec = _per_batch((1, d))
    return pl.pallas_call(
        functools.partial(_ffn_kernel, row_len=row_len),
        grid=(b, l // tb),
        in_specs=[tok, vec, vec, vec,
                  _resident((d, D_FF)), _resident((d, D_FF)),
                  _resident((3, D_FF)), _resident((3, D_FF)),
                  _resident((1, D_FF)), _resident((1, D_FF)),
                  _resident((D_FF, d)), _resident((1, d)), _resident((1, d))],
        out_specs=tok,
        out_shape=jax.ShapeDtypeStruct(h.shape, F32),
        scratch_shapes=[pltpu.VMEM((tb, D_FF), BF16)],
        compiler_params=_params("arbitrary", "arbitrary"),
        name="conv_ffn",
    )(h, mods["sh2"], mods["sc2"], mods["ga2"], w["wa"], w["wg"], w["cwa"], w["cwg"],
      w["cba"], w["cbg"], w["wd"], w["ln_g"], w["ln_b"])


def _sgu_kernel(h_ref, sh_ref, sc_ref, ga_ref, win_ref, vg_ref, vb_ref, ws_ref, bs_ref, wout_ref,
                lng_ref, lnb_ref, o_ref, v_ref, p_ref):
    h = h_ref[0]
    tb = h.shape[0]
    hm = _modulated_bf16(h, sh_ref, sc_ref)
    tiles = [slice(j, j + SGU_TILE) for j in range(0, SGU_WIDTH, SGU_TILE)]

    total = jnp.zeros((tb, 1), F32)
    for sl in tiles:
        z = _gelu(_dot(hm, win_ref[:, SGU_WIDTH + sl.start:SGU_WIDTH + sl.stop]))
        v_ref[:, sl] = z
        total = total + jnp.sum(z, axis=-1, keepdims=True)
    mu = total * (1.0 / SGU_WIDTH)
    sq = jnp.zeros((tb, 1), F32)
    for sl in tiles:
        zc = v_ref[:, sl] - mu
        sq = sq + jnp.sum(zc * zc, axis=-1, keepdims=True)
    rstd = lax.rsqrt(sq * (1.0 / SGU_WIDTH) + LN_EPS)

    for t, sl in enumerate(tiles):
        vn = ((v_ref[:, sl] - mu) * rstd * vg_ref[:, sl] + vb_ref[:, sl]).astype(BF16)
        u = _gelu(_dot(hm, win_ref[:, sl]))
        for gi in range(SGU_TILE // SGU_GW):
            g = t * (SGU_TILE // SGU_GW) + gi
            lo = slice(gi * SGU_GW, (gi + 1) * SGU_GW)
            gl = slice(g * SGU_GW, (g + 1) * SGU_GW)
            for n in range(tb // SGU_CHUNK):
                rows = slice(n * SGU_CHUNK, (n + 1) * SGU_CHUNK)
                mixed = _dot(ws_ref[g], vn[rows, lo]) + bs_ref[:, gl]
                p_ref[rows, gl] = (u[rows, lo] * mixed).astype(BF16)
    y = _dot(p_ref[...], wout_ref[...])
    o_ref[0] = _layer_norm(ALPHA * h + ga_ref[0] * y, lng_ref[...], lnb_ref[...])


def _sgu_layer(h, mods, w, tb):
    b, l, d = h.shape
    tok = pl.BlockSpec((1, tb, d), lambda i, j: (i, j, 0))
    vec = _per_batch((1, d))
    return pl.pallas_call(
        _sgu_kernel,
        grid=(b, l // tb),
        in_specs=[tok, vec, vec, vec,
                  _resident((d, 2 * SGU_WIDTH)), _resident((1, SGU_WIDTH)), _resident((1, SGU_WIDTH)),
                  _resident((SGU_GROUPS, SGU_CHUNK, SGU_CHUNK)), _resident((SGU_CHUNK, SGU_WIDTH)),
                  _resident((SGU_WIDTH, d)), _resident((1, d)), _resident((1, d))],
        out_specs=tok,
        out_shape=jax.ShapeDtypeStruct(h.shape, F32),
        scratch_shapes=[pltpu.VMEM((tb, SGU_WIDTH), F32), pltpu.VMEM((tb, SGU_WIDTH), BF16)],
        compiler_params=_params("arbitrary", "arbitrary"),
        name="chunk_sgu",
    )(h, mods["sh1"], mods["sc1"], mods["ga1"], w["w_in"], w["v_g"], w["v_b"], w["w_s"], w["b_s"],
      w["w_out"], w["ln_g"], w["ln_b"])


def _lower_bound(lb_ref, idx):
    x = lb_ref[...]
    e = jnp.exp(x - jnp.max(x, axis=0, keepdims=True))
    p = e / jnp.sum(e, axis=0, keepdims=True)
    acc = p[0:1]
    for r in range(1, idx + 1):
        acc = acc + p[r:r + 1]
    return acc - p[0:1]


def _level_half(i):
    return (SCAN_CHUNK // 2) >> i


def _segment_matrix(rev):
    c = SCAN_CHUNK
    t = np.arange(c)[:, None]
    r = np.arange(c)[None, :]
    blocks = [r >= t] if rev else [r <= t]
    for i in range(SCAN_LEVELS):
        m = _level_half(i)
        if m >= SUBLANES:
            continue
        upper = (t % (2 * m)) >= m
        if rev:
            ref = t - t % (2 * m) + m
            blocks.append(np.where(upper, (r >= ref) & (r < t), (r >= t) & (r < ref)))
        else:
            ref = t - t % (2 * m) + m - 1
            blocks.append(np.where(upper, (r > ref) & (r <= t), (r > t) & (r <= ref)))
    assert len(blocks) == SCAN_SEGMENTS
    u = np.concatenate(blocks, axis=0).astype(np.float32)
    return jnp.asarray(np.tile(u, (1, 3)), BF16)


def _pair_owner(rev):
    c = SCAN_CHUNK
    t = lax.broadcasted_iota(jnp.int32, (c, c), 0)
    s = lax.broadcasted_iota(jnp.int32, (c, c), 1)
    x = t ^ s
    owner = jnp.where(t == s, SCAN_LEVELS, -1)
    for i in range(SCAN_LEVELS):
        m = (c // 2) >> i
        owner = jnp.where((x >= m) & (x < 2 * m), i, owner)
    return jnp.where((t <= s) if rev else (t >= s), owner, -1)


def _query_rows(rev):
    row = lax.broadcasted_iota(jnp.int32, (SCAN_CHUNK, D_MODEL), 0)
    masks = []
    for i in range(SCAN_LEVELS):
        upper = (row & ((SCAN_CHUNK // 2) >> i)) != 0
        masks.append(jnp.logical_not(upper) if rev else upper)
    return masks


def _column_tiles(width):
    return [slice(j, j + PROJ_TILE) for j in range(0, width, PROJ_TILE)]


def _forget_gate(z, lb, k_ref, lg3_ref, rows, sl):
    f = lb + (1.0 - lb) * jax.nn.sigmoid(z)
    k_ref[rows, sl] = 1.0 - f
    lg = jnp.log(f) * LOG2_E
    hi = lg.astype(BF16)
    rest = lg - hi.astype(F32)
    mid = rest.astype(BF16)
    lg3_ref[0, rows, sl] = hi
    lg3_ref[1, rows, sl] = mid
    lg3_ref[2, rows, sl] = (rest - mid.astype(F32)).astype(BF16)


def _slab_level(gsum, q, k, m, rev):
    parts = []
    for start in range(0, SCAN_CHUNK, 2 * m):
        lo, hi = slice(start, start + m), slice(start + m, start + 2 * m)
        if rev:
            ref = gsum[start + m:start + m + 1]
            parts += [q[lo] * jnp.exp2(gsum[lo] - ref), k[hi] * jnp.exp2(ref - gsum[hi])]
        else:
            ref = gsum[start + m - 1:start + m]
            parts += [k[lo] * jnp.exp2(ref - gsum[lo]), q[hi] * jnp.exp2(gsum[hi] - ref)]
    return jnp.concatenate(parts, axis=0)


def _scan_operands(ci, q_ref, k_ref, lg3_ref, u_ref, x_ref, dec_ref, diag_ref, rev):
    c = SCAN_CHUNK
    query_rows = _query_rows(rev)
    lane = lax.broadcasted_iota(jnp.int32, (c, LANES), 1)
    rows = _chunk_rows(ci)
    lg3 = jnp.concatenate([lg3_ref[0, rows, :], lg3_ref[1, rows, :], lg3_ref[2, rows, :]], axis=0)
    seg = _dot(u_ref[...], lg3)
    gsum = seg[0:c]
    q = q_ref[0, rows, :]
    k = k_ref[rows, :]
    block = 1
    for i in range(SCAN_LEVELS):
        if _level_half(i) >= SUBLANES:
            x = _slab_level(gsum, q, k, _level_half(i), rev)
        else:
            x = jnp.where(query_rows[i], q, k) * jnp.exp2(seg[block * c:(block + 1) * c])
            block += 1
        x_ref[i, rows, :] = x.astype(BF16)
    g_end = gsum[0:1] if rev else gsum[c - 1:c]
    x_ref[SCAN_LEVELS, rows, :] = (q * jnp.exp2(gsum)).astype(BF16)
    x_ref[SCAN_LEVELS + 1, rows, :] = (k * jnp.exp2(g_end - gsum)).astype(BF16)
    dec_ref[ci] = jnp.exp2(g_end)
    qk = q * k
    diag = jnp.zeros((c, LANES), F32)
    for hd in range(HG_HEADS):
        head_sum = jnp.sum(qk[:, hd * HG_DK:(hd + 1) * HG_DK], axis=-1, keepdims=True)
        diag = jnp.where(lane == hd, head_sum, diag)
    diag_ref[rows, :] = diag


def _chunk_rows(ci):
    return slice(ci * SCAN_CHUNK, (ci + 1) * SCAN_CHUNK)


def _scan_block(tb, rev, project, operands, step, finish=None):
    group_rows = min(tb, PROJ_ROWS)
    per_group = group_rows // SCAN_CHUNK
    starts = list(range(0, tb, group_rows))[::-1] if rev else list(range(0, tb, group_rows))
    groups = []
    for r0 in starts:
        chunks = [r0 // SCAN_CHUNK + j for j in range(per_group)]
        groups.append((slice(r0, r0 + group_rows), chunks[::-1] if rev else chunks))
    for thunk in project(groups[0][0]):
        thunk()
    operands(groups[0][1][0])
    for g, (rows, chunks) in enumerate(groups):
        fillers = list(project(groups[g + 1][0])) if g + 1 < len(groups) else []
        if finish is not None and g > 0:
            fillers.insert(0, functools.partial(finish, groups[g - 1][0]))
        slots = max(per_group - 1, 1)
        for j, ci in enumerate(chunks):
            take = -(-len(fillers) // (slots - j)) if j < slots else len(fillers)
            for thunk in fillers[:take]:
                thunk()
            fillers = fillers[take:]
            if j + 1 < per_group:
                operands(chunks[j + 1])
            elif g + 1 < len(groups):
                operands(groups[g + 1][1][0])
            step(ci)
    if finish is not None:
        finish(groups[-1][0])


def _chunk_scores(x_ref, diag_ref, rows, owner):
    diag = diag_ref[rows, :]
    scores = []
    for hd in range(HG_HEADS):
        cols = slice(hd * HG_DK, (hd + 1) * HG_DK)
        s = jnp.where(owner == SCAN_LEVELS, diag[:, hd:hd + 1], 0.0)
        for i in range(SCAN_LEVELS):
            x = x_ref[i, rows, cols]
            s = jnp.where(owner == i, lax.dot_general(x, x, _NT, preferred_element_type=F32), s)
        scores.append(s.astype(BF16))
    return scores


def _scan_step(x_ref, dec, rows, cols, scores, vb, st):
    o = _dot(scores, vb) + _dot(x_ref[SCAN_LEVELS, rows, cols], st.astype(BF16))
    dec_rows = jnp.broadcast_to(dec, (HG_DK, HG_DK)).T
    st_new = st * dec_rows + lax.dot_general(x_ref[SCAN_LEVELS + 1, rows, cols], vb, _TN,
                                             preferred_element_type=F32)
    return o, st_new


def _hg_fwd_kernel(h_ref, sh_ref, sc_ref, wq_ref, wi_ref, wf_ref, lb_ref, u_ref, s0_ref,
                   o_ref, q_ref, v_ref, st_ref, k_scr, lg3_scr, x_scr, dec_scr, diag_scr, *, lb_idx):
    @pl.when(pl.program_id(1) == 0)
    def _():
        st_ref[...] = s0_ref[...]

    tb = h_ref.shape[1]
    lb = _lower_bound(lb_ref, lb_idx)

    def project(rows):
        hm = _modulated_bf16(h_ref[0, rows, :], sh_ref, sc_ref)

        def tile(sl):
            _forget_gate(_dot(hm, wf_ref[:, sl]), lb[:, sl], k_scr, lg3_scr, rows, sl)
            q_ref[0, rows, sl] = _silu(_dot(hm, wq_ref[:, sl]))
            v_ref[0, rows, sl] = _dot(hm, wi_ref[:, sl]).astype(BF16)

        return [functools.partial(tile, sl) for sl in _column_tiles(h_ref.shape[2])]

    owner = _pair_owner(False)

    def operands(ci):
        _scan_operands(ci, q_ref, k_scr, lg3_scr, u_ref, x_scr, dec_scr, diag_scr, False)

    def step(ci):
        rows = _chunk_rows(ci)
        dec = dec_scr[ci]
        scores = _chunk_scores(x_scr, diag_scr, rows, owner)
        for hd in range(HG_HEADS):
            cols = slice(hd * HG_DK, (hd + 1) * HG_DK)
            o, st = _scan_step(x_scr, dec[:, cols], rows, cols, scores[hd],
                               v_ref[0, rows, cols], st_ref[0, hd])
            o_ref[0, rows, cols] = o
            st_ref[0, hd] = st

    _scan_block(tb, False, project, operands, step)


def _hg_bwd_kernel(h_ref, sh_ref, sc_ref, ga_ref, q_ref, v_ref, of_ref, wg_ref, wf_ref, lb_ref,
                   nw_ref, wo_ref, lng_ref, lnb_ref, u_ref, s0_ref,
                   hn_ref, st_ref, k_scr, lg3_scr, x_scr, dec_scr, diag_scr, g_scr, y_scr, *, lb_idx):
    @pl.when(pl.program_id(1) == 0)
    def _():
        st_ref[...] = s0_ref[...]

    tb = h_ref.shape[1]
    lb = _lower_bound(lb_ref, lb_idx)

    def project(rows):
        hm = _modulated_bf16(h_ref[0, rows, :], sh_ref, sc_ref)

        def tile(sl):
            _forget_gate(_dot(hm, wf_ref[:, sl]), lb[:, sl], k_scr, lg3_scr, rows, sl)
            g_scr[rows, sl] = _silu(_dot(hm, wg_ref[:, sl]))

        return [functools.partial(tile, sl) for sl in _column_tiles(h_ref.shape[2])]

    def finish(rows):
        y = _dot(y_scr[rows, :], wo_ref[...])
        hn_ref[0, rows, :] = _layer_norm(ALPHA * h_ref[0, rows, :] + ga_ref[0] * y,
                                         lng_ref[...], lnb_ref[...])

    owner = _pair_owner(True)
    norm_w = nw_ref[...]

    def operands(ci):
        _scan_operands(ci, q_ref, k_scr, lg3_scr, u_ref, x_scr, dec_scr, diag_scr, True)

    def step(ci):
        rows = _chunk_rows(ci)
        dec = dec_scr[ci]
        scores = _chunk_scores(x_scr, diag_scr, rows, owner)
        for hd in range(HG_HEADS):
            cols = slice(hd * HG_DK, (hd + 1) * HG_DK)
            o, st = _scan_step(x_scr, dec[:, cols], rows, cols, scores[hd],
                               v_ref[0, rows, cols], st_ref[0, hd])
            st_ref[0, hd] = st
            o = o + of_ref[0, rows, cols]
            o = o * lax.rsqrt(jnp.mean(o * o, axis=-1, keepdims=True) + RMS_EPS) * norm_w
            y_scr[rows, cols] = (o * g_scr[rows, cols]).astype(BF16)

    _scan_block(tb, True, project, operands, step, finish)


def _hgrn_layer(h, mods, w, s0_f, s0_b, tb):
    b, l, d = h.shape
    nb = l // tb
    vec = _per_batch((1, d))
    state = _per_batch((HG_HEADS, HG_DK, HG_DK))
    state_shape = jax.ShapeDtypeStruct((b, HG_HEADS, HG_DK, HG_DK), F32)
    tok_shape = jax.ShapeDtypeStruct(h.shape, F32)
    n_lb = w["lb_f"].shape[0]
    seg = _resident((SCAN_SEGMENTS * SCAN_CHUNK, 3 * SCAN_CHUNK))
    scan_scratch = [pltpu.VMEM((tb, d), F32),
                    pltpu.VMEM((3, tb, d), BF16),
                    pltpu.VMEM((SCAN_OPERANDS, tb, d), BF16),
                    pltpu.VMEM((tb // SCAN_CHUNK, 1, d), F32),
                    pltpu.VMEM((tb, LANES), F32)]

    tok = pl.BlockSpec((1, tb, d), lambda i, j: (i, j, 0))
    o_f, q, v, s_f = pl.pallas_call(
        functools.partial(_hg_fwd_kernel, lb_idx=w["lb_idx"]),
        grid=(b, nb),
        in_specs=[tok, vec, vec, _resident((d, d)), _resident((d, d)), _resident((d, d)),
                  _resident((n_lb, d)), seg, state],
        out_specs=[tok, tok, tok, state],
        out_shape=[tok_shape, tok_shape, jax.ShapeDtypeStruct(h.shape, BF16), state_shape],
        scratch_shapes=scan_scratch,
        compiler_params=_params("arbitrary", "arbitrary"),
        name="hgrn2_forward_scan",
    )(h, mods["sh1"], mods["sc1"], w["wq"], w["wi"], w["wff"], w["lb_f"], _segment_matrix(False),
      s0_f)

    rtok = pl.BlockSpec((1, tb, d), lambda i, j: (i, nb - 1 - j, 0))
    h_new, s_b = pl.pallas_call(
        functools.partial(_hg_bwd_kernel, lb_idx=w["lb_idx"]),
        grid=(b, nb),
        in_specs=[rtok, vec, vec, vec, rtok, rtok, rtok,
                  _resident((d, d)), _resident((d, d)), _resident((n_lb, d)),
                  _resident((1, HG_DK)), _resident((d, d)), _resident((1, d)), _resident((1, d)),
                  seg, state],
        out_specs=[rtok, state],
        out_shape=[tok_shape, state_shape],
        scratch_shapes=scan_scratch + [pltpu.VMEM((tb, d), F32), pltpu.VMEM((tb, d), BF16)],
        compiler_params=_params("arbitrary", "arbitrary"),
        name="hgrn2_backward_scan_readout",
    )(h, mods["sh1"], mods["sc1"], mods["ga1"], q, v, o_f, w["wg"], w["wfb"], w["lb_b"],
      w["norm_w"], w["w_out"], w["ln_g"], w["ln_b"], _segment_matrix(True), s0_b)
    return h_new, s_f, s_b


def _token_block(l):
    for tb in (512, 256, 128):
        if l % tb == 0:
            return tb
    raise ValueError(f"sequence length {l} is not a multiple of 128")


def kernel(x, c, ctx, c_ctx, ada_w, ada_b, ln_g, ln_b, hg_w_in, hg_lb, hg_norm_w, hg_w_out, sgu_w_in, sgu_ln_g, sgu_ln_b, sgu_w_s, sgu_b_s, sgu_w_out, ffn_w_up, ffn_conv_w, ffn_conv_b, ffn_w_down):
    batch, seq, d = x.shape
    ctx_len = ctx.shape[1]
    assert d == D_MODEL and batch + 1 <= ADA_ROWS
    tb_x, tb_c = _token_block(seq), _token_block(ctx_len)
    assert tb_x % GRID_W == 0 and tb_c == ctx_len

    cc = jnp.concatenate([c, c_ctx[None], jnp.zeros((ADA_ROWS - batch - 1, d), F32)], axis=0)
    ada = _ada_modulation(cc, ada_w, ada_b)
    names = ("sh1", "sc1", "ga1", "sh2", "sc2", "ga2")

    def mods_for(layer, is_ctx):
        m = ada[layer].reshape(ADA_ROWS, 6, 1, d)
        if is_ctx:
            rows = jnp.broadcast_to(m[batch:batch + 1], (batch, 6, 1, d))
        else:
            rows = m[:batch]
        return {n: rows[:, i] for i, n in enumerate(names)}

    k1, k2 = HG_HEADS * HG_DK, 2 * HG_HEADS * HG_DK
    zero_state = jnp.zeros((batch, HG_HEADS, HG_DK, HG_DK), F32)
    h, hc = x, ctx
    for layer in range(DEPTH):
        kind, idx = layer % N_MIXERS, layer // N_MIXERS
        ctx_later = any(j % N_MIXERS == 0 for j in range(layer + 1, DEPTH))
        mx = mods_for(layer, False)
        mc = mods_for(layer, True) if (kind == 0 or ctx_later) else None
        ln1 = {"ln_g": ln_g[layer, 0][None], "ln_b": ln_b[layer, 0][None]}
        if kind == 0:
            wi = hg_w_in[idx].astype(BF16)
            w = dict(ln1, wq=wi[:, :k1], wg=wi[:, k1:k2], wi=wi[:, k2:k2 + k1],
                     wff=wi[:, k2 + k1:k2 + 2 * k1], wfb=wi[:, k2 + 2 * k1:],
                     lb_f=hg_lb[0], lb_b=hg_lb[1], lb_idx=idx,
                     norm_w=hg_norm_w[idx][None], w_out=hg_w_out[idx].astype(BF16))
            hc_new, s_f, s_b = _hgrn_layer(hc, mc, w, zero_state, zero_state, tb_c)
            h, _, _ = _hgrn_layer(h, mx, w, s_f, s_b, min(tb_x, PROJ_ROWS))
            if ctx_later:
                hc = hc_new
        else:
            w = dict(ln1, w_in=sgu_w_in[idx].astype(BF16), v_g=sgu_ln_g[idx][None],
                     v_b=sgu_ln_b[idx][None], w_s=sgu_w_s[idx].astype(BF16),
                     b_s=jnp.repeat(sgu_b_s[idx].T, SGU_GW, axis=1),
                     w_out=sgu_w_out[idx].astype(BF16))
            h = _sgu_layer(h, mx, w, tb_x)
            if ctx_later:
                hc = _sgu_layer(hc, mc, w, tb_c)
        up = ffn_w_up[layer].astype(BF16)
        w = dict(wa=up[:, :D_FF], wg=up[:, D_FF:], cwa=ffn_conv_w[layer][:, :D_FF],
                 cwg=ffn_conv_w[layer][:, D_FF:], cba=ffn_conv_b[layer][None, :D_FF],
                 cbg=ffn_conv_b[layer][None, D_FF:], wd=ffn_w_down[layer].astype(BF16),
                 ln_g=ln_g[layer, 1][None], ln_b=ln_b[layer, 1][None])
        h = _ffn_layer(h, mx, w, GRID_W, tb_x)
        if ctx_later:
            hc = _ffn_layer(hc, mc, w, ctx_len, tb_c)
    return h
```

```python
import functools

import jax
import jax.numpy as jnp
import numpy as np
from jax import lax
from jax.experimental import pallas as pl
from jax.experimental.pallas import tpu as pltpu

F32 = jnp.float32
BF16 = jnp.bfloat16

LANES = 128
D_MODEL = 1024
DEPTH = 4
N_MIXERS = 2
GRID_W = 64
HG_DK = LANES
HG_HEADS = D_MODEL // HG_DK
HG_COL_Q, HG_COL_GATE, HG_COL_V, HG_COL_F_FWD, HG_COL_F_BWD = (i * D_MODEL for i in range(5))
HG_IN = 5 * D_MODEL
SCAN_CHUNK = 64
SCAN_LEVELS = 6
SUBLANES = 8
SCAN_SEGMENTS = 4
SCAN_OPERANDS = SCAN_LEVELS + 2
LOG2_E = 1.4426950408889634
SGU_CHUNK = 128
SGU_WIDTH = 3 * D_MODEL
SGU_GROUPS = 8
SGU_GW = SGU_WIDTH // SGU_GROUPS
SGU_TILE = 2 * SGU_GW
D_FF = 128 * ((8 * D_MODEL // 3 + 127) // 128)
FF_TILE = 256
PROJ_TILE = 256
ALPHA = (2.0 * DEPTH) ** 0.25
LN_EPS = 1e-5
RMS_EPS = 1e-6
VMEM_LIMIT_BYTES = 56 * 1024 * 1024
ADA_TILE = 1536
ADA_ROWS = 8

_NT = (((1,), (1,)), ((), ()))
_TN = (((0,), (0,)), ((), ()))


def _params(*sem):
    return pltpu.CompilerParams(dimension_semantics=sem, vmem_limit_bytes=VMEM_LIMIT_BYTES)


def _resident(shape):
    zeros = (0,) * len(shape)
    return pl.BlockSpec(shape, lambda *_: zeros, pipeline_mode=pl.Buffered(1))


def _resident_layer(shape, layer):
    zeros = (0,) * len(shape)
    return pl.BlockSpec((None,) + tuple(shape), lambda *_: (layer,) + zeros,
                        pipeline_mode=pl.Buffered(1))


def _per_batch(shape):
    zeros = (0,) * len(shape)
    return pl.BlockSpec((1,) + tuple(shape), lambda b, j: (b,) + zeros)


def _dot(a, b):
    return jnp.dot(a, b, preferred_element_type=F32)


def _silu(x):
    return x * jax.nn.sigmoid(x)


def _gelu(x):
    return 0.5 * x * (1.0 + lax.erf(x * (2.0 ** -0.5)))


def _layer_norm(x, g, b):
    mu = jnp.mean(x, axis=-1, keepdims=True)
    xc = x - mu
    var = jnp.mean(xc * xc, axis=-1, keepdims=True)
    return xc * lax.rsqrt(var + LN_EPS) * g + b


def _modulated_bf16(h, sh_ref, sc_ref):
    return (h * (1.0 + sc_ref[0]) + sh_ref[0]).astype(BF16)


def _ada_kernel(c_ref, w_ref, b_ref, o_ref):
    s = _silu(c_ref[...])
    o_ref[0] = jnp.dot(s, w_ref[0], precision=lax.Precision.HIGHEST,
                       preferred_element_type=F32) + b_ref[0]


def _ada_modulation(cc, ada_w, ada_b):
    n = ada_w.shape[-1]
    return pl.pallas_call(
        _ada_kernel,
        grid=(DEPTH, n // ADA_TILE),
        in_specs=[pl.BlockSpec((ADA_ROWS, D_MODEL), lambda l, j: (0, 0)),
                  pl.BlockSpec((1, D_MODEL, ADA_TILE), lambda l, j: (l, 0, j)),
                  pl.BlockSpec((1, 1, ADA_TILE), lambda l, j: (l, 0, j))],
        out_specs=pl.BlockSpec((1, ADA_ROWS, ADA_TILE), lambda l, j: (l, 0, j)),
        out_shape=jax.ShapeDtypeStruct((DEPTH, ADA_ROWS, n), F32),
        compiler_params=_params("arbitrary", "arbitrary"),
        name="ada_modulation",
    )(cc, ada_w, ada_b.reshape(DEPTH, 1, n))


def _ffn_kernel(h_ref, sh_ref, sc_ref, ga_ref, wu_ref, cw_ref, cb_ref, wd_ref, lng_ref, lnb_ref,
                o_ref, act_ref, *, row_len):
    h = h_ref[0]
    tb = h.shape[0]
    hm = _modulated_bf16(h, sh_ref, sc_ref)
    col = lax.broadcasted_iota(jnp.int32, (tb, FF_TILE), 0) % row_len
    has_prev = col != 0
    has_next = col != row_len - 1

    def conv(sl):
        y = _dot(hm, wu_ref[:, sl])
        y_prev = jnp.where(has_prev, pltpu.roll(y, 1, 0), 0.0)
        y_next = jnp.where(has_next, pltpu.roll(y, tb - 1, 0), 0.0)
        return (y_prev * cw_ref[0:1, sl] + y * cw_ref[1:2, sl] + y_next * cw_ref[2:3, sl]
                + cb_ref[:, sl])

    for j in range(D_FF // FF_TILE):
        sl = slice(j * FF_TILE, (j + 1) * FF_TILE)
        a = conv(sl)
        g = conv(slice(D_FF + sl.start, D_FF + sl.stop))
        act_ref[:, sl] = (_silu(g) * a).astype(BF16)
    f = _dot(act_ref[...], wd_ref[...])
    o_ref[0] = _layer_norm(ALPHA * h + ga_ref[0] * f, lng_ref[...], lnb_ref[...])


def _ffn_layer(h, mods, w, layer, row_len, tb):
    b, l, d = h.shape
    tok = pl.BlockSpec((1, tb, d), lambda i, j: (i, j, 0))
    vec = _per_batch((1, d))
    return pl.pallas_call(
        functools.partial(_ffn_kernel, row_len=row_len),
        grid=(b, l // tb),
        in_specs=[tok, vec, vec, vec,
                  _resident_layer((d, 2 * D_FF), layer), _resident_layer((3, 2 * D_FF), layer),
                  _resident_layer((1, 2 * D_FF), layer), _resident_layer((D_FF, d), layer),
                  _resident((1, d)), _resident((1, d))],
        out_specs=tok,
        out_shape=jax.ShapeDtypeStruct(h.shape, F32),
        scratch_shapes=[pltpu.VMEM((tb, D_FF), BF16)],
        compiler_params=_params("arbitrary", "arbitrary"),
        name="conv_ffn",
    )(h, mods["sh2"], mods["sc2"], mods["ga2"], w["w_up"], w["conv_w"], w["conv_b"], w["w_down"],
      w["ln_g"], w["ln_b"])


def _sgu_kernel(h_ref, sh_ref, sc_ref, ga_ref, win_ref, vg_ref, vb_ref, ws_ref, bs_ref, wout_ref,
                lng_ref, lnb_ref, o_ref, v_ref, p_ref):
    h = h_ref[0]
    tb = h.shape[0]
    hm = _modulated_bf16(h, sh_ref, sc_ref)
    tiles = [slice(j, j + SGU_TILE) for j in range(0, SGU_WIDTH, SGU_TILE)]

    total = jnp.zeros((tb, 1), F32)
    for sl in tiles:
        z = _gelu(_dot(hm, win_ref[:, SGU_WIDTH + sl.start:SGU_WIDTH + sl.stop]))
        v_ref[:, sl] = z
        total = total + jnp.sum(z, axis=-1, keepdims=True)
    mu = total * (1.0 / SGU_WIDTH)
    sq = jnp.zeros((tb, 1), F32)
    for sl in tiles:
        zc = v_ref[:, sl] - mu
        sq = sq + jnp.sum(zc * zc, axis=-1, keepdims=True)
    rstd = lax.rsqrt(sq * (1.0 / SGU_WIDTH) + LN_EPS)

    for t, sl in enumerate(tiles):
        vn = ((v_ref[:, sl] - mu) * rstd * vg_ref[:, sl] + vb_ref[:, sl]).astype(BF16)
        u = _gelu(_dot(hm, win_ref[:, sl]))
        for gi in range(SGU_TILE // SGU_GW):
            g = t * (SGU_TILE // SGU_GW) + gi
            lo = slice(gi * SGU_GW, (gi + 1) * SGU_GW)
            gl = slice(g * SGU_GW, (g + 1) * SGU_GW)
            for n in range(tb // SGU_CHUNK):
                rows = slice(n * SGU_CHUNK, (n + 1) * SGU_CHUNK)
                mixed = _dot(ws_ref[g], vn[rows, lo]) + bs_ref[:, gl]
                p_ref[rows, gl] = (u[rows, lo] * mixed).astype(BF16)
    y = _dot(p_ref[...], wout_ref[...])
    o_ref[0] = _layer_norm(ALPHA * h + ga_ref[0] * y, lng_ref[...], lnb_ref[...])


def _sgu_layer(h, mods, w, idx, tb):
    b, l, d = h.shape
    tok = pl.BlockSpec((1, tb, d), lambda i, j: (i, j, 0))
    vec = _per_batch((1, d))
    return pl.pallas_call(
        _sgu_kernel,
        grid=(b, l // tb),
        in_specs=[tok, vec, vec, vec,
                  _resident_layer((d, 2 * SGU_WIDTH), idx), _resident((1, SGU_WIDTH)),
                  _resident((1, SGU_WIDTH)),
                  _resident((SGU_GROUPS, SGU_CHUNK, SGU_CHUNK)), _resident((SGU_CHUNK, SGU_WIDTH)),
                  _resident_layer((SGU_WIDTH, d), idx), _resident((1, d)), _resident((1, d))],
        out_specs=tok,
        out_shape=jax.ShapeDtypeStruct(h.shape, F32),
        scratch_shapes=[pltpu.VMEM((tb, SGU_WIDTH), F32), pltpu.VMEM((tb, SGU_WIDTH), BF16)],
        compiler_params=_params("arbitrary", "arbitrary"),
        name="chunk_sgu",
    )(h, mods["sh1"], mods["sc1"], mods["ga1"], w["w_in"], w["v_g"], w["v_b"], w["w_s"], w["b_s"],
      w["w_out"], w["ln_g"], w["ln_b"])


def _lower_bound(lb_ref, idx):
    x = lb_ref[...]
    e = jnp.exp(x - jnp.max(x, axis=0, keepdims=True))
    p = e / jnp.sum(e, axis=0, keepdims=True)
    acc = p[0:1]
    for r in range(1, idx + 1):
        acc = acc + p[r:r + 1]
    return acc - p[0:1]


def _level_half(i):
    return (SCAN_CHUNK // 2) >> i


def _segment_matrix(rev):
    c = SCAN_CHUNK
    t = np.arange(c)[:, None]
    r = np.arange(c)[None, :]
    blocks = [r >= t] if rev else [r <= t]
    for i in range(SCAN_LEVELS):
        m = _level_half(i)
        if m >= SUBLANES:
            continue
        upper = (t % (2 * m)) >= m
        if rev:
            ref = t - t % (2 * m) + m
            blocks.append(np.where(upper, (r >= ref) & (r < t), (r >= t) & (r < ref)))
        else:
            ref = t - t % (2 * m) + m - 1
            blocks.append(np.where(upper, (r > ref) & (r <= t), (r > t) & (r <= ref)))
    assert len(blocks) == SCAN_SEGMENTS
    u = np.concatenate(blocks, axis=0).astype(np.float32)
    return jnp.asarray(np.tile(u, (1, 3)), BF16)


def _pair_owner(rev):
    c = SCAN_CHUNK
    t = lax.broadcasted_iota(jnp.int32, (c, c), 0)
    s = lax.broadcasted_iota(jnp.int32, (c, c), 1)
    x = t ^ s
    owner = jnp.where(t == s, SCAN_LEVELS, -1)
    for i in range(SCAN_LEVELS):
        m = (c // 2) >> i
        owner = jnp.where((x >= m) & (x < 2 * m), i, owner)
    return jnp.where((t <= s) if rev else (t >= s), owner, -1)


def _query_rows(rev):
    row = lax.broadcasted_iota(jnp.int32, (SCAN_CHUNK, D_MODEL), 0)
    masks = []
    for i in range(SCAN_LEVELS):
        upper = (row & ((SCAN_CHUNK // 2) >> i)) != 0
        masks.append(jnp.logical_not(upper) if rev else upper)
    return masks


def _shift(sl, offset):
    return slice(sl.start + offset, sl.stop + offset)


def _column_tiles(width):
    return [slice(j, j + PROJ_TILE) for j in range(0, width, PROJ_TILE)]


def _forget_gate(z, lb, k_ref, lg3_ref, sl):
    f = lb + (1.0 - lb) * jax.nn.sigmoid(z)
    k_ref[:, sl] = 1.0 - f
    lg = jnp.log(f) * LOG2_E
    hi = lg.astype(BF16)
    rest = lg - hi.astype(F32)
    mid = rest.astype(BF16)
    lg3_ref[0, :, sl] = hi
    lg3_ref[1, :, sl] = mid
    lg3_ref[2, :, sl] = (rest - mid.astype(F32)).astype(BF16)


def _slab_level(gsum, q, k, m, rev):
    parts = []
    for start in range(0, SCAN_CHUNK, 2 * m):
        lo, hi = slice(start, start + m), slice(start + m, start + 2 * m)
        if rev:
            ref = gsum[start + m:start + m + 1]
            parts += [q[lo] * jnp.exp2(gsum[lo] - ref), k[hi] * jnp.exp2(ref - gsum[hi])]
        else:
            ref = gsum[start + m - 1:start + m]
            parts += [k[lo] * jnp.exp2(ref - gsum[lo]), q[hi] * jnp.exp2(gsum[hi] - ref)]
    return jnp.concatenate(parts, axis=0)


def _scan_operands(ci, q_ref, k_ref, lg3_ref, u_ref, x_ref, dec_ref, diag_ref, rev):
    c = SCAN_CHUNK
    query_rows = _query_rows(rev)
    lane = lax.broadcasted_iota(jnp.int32, (c, LANES), 1)
    rows = _chunk_rows(ci)
    lg3 = jnp.concatenate([lg3_ref[0, rows, :], lg3_ref[1, rows, :], lg3_ref[2, rows, :]], axis=0)
    seg = _dot(u_ref[...], lg3)
    gsum = seg[0:c]
    q = q_ref[0, rows, :]
    k = k_ref[rows, :]
    block = 1
    for i in range(SCAN_LEVELS):
        if _level_half(i) >= SUBLANES:
            x = _slab_level(gsum, q, k, _level_half(i), rev)
        else:
            x = jnp.where(query_rows[i], q, k) * jnp.exp2(seg[block * c:(block + 1) * c])
            block += 1
        x_ref[i, rows, :] = x.astype(BF16)
    g_end = gsum[0:1] if rev else gsum[c - 1:c]
    x_ref[SCAN_LEVELS, rows, :] = (q * jnp.exp2(gsum)).astype(BF16)
    x_ref[SCAN_LEVELS + 1, rows, :] = (k * jnp.exp2(g_end - gsum)).astype(BF16)
    dec_ref[ci] = jnp.exp2(g_end)
    qk = q * k
    diag = jnp.zeros((c, LANES), F32)
    for hd in range(HG_HEADS):
        head_sum = jnp.sum(qk[:, hd * HG_DK:(hd + 1) * HG_DK], axis=-1, keepdims=True)
        diag = jnp.where(lane == hd, head_sum, diag)
    diag_ref[rows, :] = diag


def _chunk_rows(ci):
    return slice(ci * SCAN_CHUNK, (ci + 1) * SCAN_CHUNK)


def _scan_block(n_chunks, rev, operands, step):
    order = list(range(n_chunks))[::-1] if rev else list(range(n_chunks))
    operands(order[0])
    for i, ci in enumerate(order):
        if i + 1 < n_chunks:
            operands(order[i + 1])
        step(ci)


def _chunk_scores(x_ref, diag_ref, rows, owner):
    diag = diag_ref[rows, :]
    scores = []
    for hd in range(HG_HEADS):
        cols = slice(hd * HG_DK, (hd + 1) * HG_DK)
        s = jnp.where(owner == SCAN_LEVELS, diag[:, hd:hd + 1], 0.0)
        for i in range(SCAN_LEVELS):
            x = x_ref[i, rows, cols]
            s = jnp.where(owner == i, lax.dot_general(x, x, _NT, preferred_element_type=F32), s)
        scores.append(s.astype(BF16))
    return scores


def _scan_step(x_ref, dec, rows, cols, scores, vb, st):
    o = _dot(scores, vb) + _dot(x_ref[SCAN_LEVELS, rows, cols], st.astype(BF16))
    dec_rows = jnp.broadcast_to(dec, (HG_DK, HG_DK)).T
    st_new = st * dec_rows + lax.dot_general(x_ref[SCAN_LEVELS + 1, rows, cols], vb, _TN,
                                             preferred_element_type=F32)
    return o, st_new


def _hg_fwd_kernel(h_ref, sh_ref, sc_ref, win_ref, lb_ref, u_ref, s0_ref,
                   o_ref, q_ref, v_ref, st_ref, k_scr, lg3_scr, x_scr, dec_scr, diag_scr, *, lb_idx):
    @pl.when(pl.program_id(1) == 0)
    def _():
        st_ref[...] = s0_ref[...]

    h = h_ref[0]
    tb = h.shape[0]
    hm = _modulated_bf16(h, sh_ref, sc_ref)
    lb = _lower_bound(lb_ref, lb_idx)
    for sl in _column_tiles(h.shape[1]):
        _forget_gate(_dot(hm, win_ref[:, _shift(sl, HG_COL_F_FWD)]), lb[:, sl], k_scr, lg3_scr, sl)
        q_ref[0, :, sl] = _silu(_dot(hm, win_ref[:, _shift(sl, HG_COL_Q)]))
        v_ref[0, :, sl] = _dot(hm, win_ref[:, _shift(sl, HG_COL_V)]).astype(BF16)
    owner = _pair_owner(False)

    def operands(ci):
        _scan_operands(ci, q_ref, k_scr, lg3_scr, u_ref, x_scr, dec_scr, diag_scr, False)

    def step(ci):
        rows = _chunk_rows(ci)
        dec = dec_scr[ci]
        scores = _chunk_scores(x_scr, diag_scr, rows, owner)
        for hd in range(HG_HEADS):
            cols = slice(hd * HG_DK, (hd + 1) * HG_DK)
            o, st = _scan_step(x_scr, dec[:, cols], rows, cols, scores[hd],
                               v_ref[0, rows, cols], st_ref[0, hd])
            o_ref[0, rows, cols] = o
            st_ref[0, hd] = st

    _scan_block(tb // SCAN_CHUNK, False, operands, step)


def _hg_bwd_kernel(h_ref, sh_ref, sc_ref, ga_ref, q_ref, v_ref, of_ref, win_ref, lb_ref,
                   nw_ref, wo_ref, lng_ref, lnb_ref, u_ref, s0_ref,
                   hn_ref, st_ref, k_scr, lg3_scr, x_scr, dec_scr, diag_scr, g_scr, y_scr, *, lb_idx):
    @pl.when(pl.program_id(1) == 0)
    def _():
        st_ref[...] = s0_ref[...]

    h = h_ref[0]
    tb = h.shape[0]
    hm = _modulated_bf16(h, sh_ref, sc_ref)
    lb = _lower_bound(lb_ref, lb_idx)
    for sl in _column_tiles(h.shape[1]):
        _forget_gate(_dot(hm, win_ref[:, _shift(sl, HG_COL_F_BWD)]), lb[:, sl], k_scr, lg3_scr, sl)
        g_scr[:, sl] = _silu(_dot(hm, win_ref[:, _shift(sl, HG_COL_GATE)]))
    owner = _pair_owner(True)
    norm_w = nw_ref[...]

    def operands(ci):
        _scan_operands(ci, q_ref, k_scr, lg3_scr, u_ref, x_scr, dec_scr, diag_scr, True)

    def step(ci):
        rows = _chunk_rows(ci)
        dec = dec_scr[ci]
        scores = _chunk_scores(x_scr, diag_scr, rows, owner)
        for hd in range(HG_HEADS):
            cols = slice(hd * HG_DK, (hd + 1) * HG_DK)
            o, st = _scan_step(x_scr, dec[:, cols], rows, cols, scores[hd],
                               v_ref[0, rows, cols], st_ref[0, hd])
            st_ref[0, hd] = st
            o = o + of_ref[0, rows, cols]
            o = o * lax.rsqrt(jnp.mean(o * o, axis=-1, keepdims=True) + RMS_EPS) * norm_w
            y_scr[rows, cols] = (o * g_scr[rows, cols]).astype(BF16)

    _scan_block(tb // SCAN_CHUNK, True, operands, step)
    y = _dot(y_scr[...], wo_ref[...])
    hn_ref[0] = _layer_norm(ALPHA * h + ga_ref[0] * y, lng_ref[...], lnb_ref[...])


def _hgrn_layer(h, mods, w, idx, s0_f, s0_b, tb):
    b, l, d = h.shape
    nb = l // tb
    vec = _per_batch((1, d))
    state = _per_batch((HG_HEADS, HG_DK, HG_DK))
    state_shape = jax.ShapeDtypeStruct((b, HG_HEADS, HG_DK, HG_DK), F32)
    tok_shape = jax.ShapeDtypeStruct(h.shape, F32)
    n_lb = w["lb_f"].shape[0]
    seg = _resident((SCAN_SEGMENTS * SCAN_CHUNK, 3 * SCAN_CHUNK))
    scan_scratch = [pltpu.VMEM((tb, d), F32),
                    pltpu.VMEM((3, tb, d), BF16),
                    pltpu.VMEM((SCAN_OPERANDS, tb, d), BF16),
                    pltpu.VMEM((tb // SCAN_CHUNK, 1, d), F32),
                    pltpu.VMEM((tb, LANES), F32)]

    tok = pl.BlockSpec((1, tb, d), lambda i, j: (i, j, 0))
    o_f, q, v, s_f = pl.pallas_call(
        functools.partial(_hg_fwd_kernel, lb_idx=idx),
        grid=(b, nb),
        in_specs=[tok, vec, vec, _resident_layer((d, HG_IN), idx), _resident((n_lb, d)), seg, state],
        out_specs=[tok, tok, tok, state],
        out_shape=[tok_shape, tok_shape, jax.ShapeDtypeStruct(h.shape, BF16), state_shape],
        scratch_shapes=scan_scratch,
        compiler_params=_params("arbitrary", "arbitrary"),
        name="hgrn2_forward_scan",
    )(h, mods["sh1"], mods["sc1"], w["w_in"], w["lb_f"], _segment_matrix(False), s0_f)

    rtok = pl.BlockSpec((1, tb, d), lambda i, j: (i, nb - 1 - j, 0))
    h_new, s_b = pl.pallas_call(
        functools.partial(_hg_bwd_kernel, lb_idx=idx),
        grid=(b, nb),
        in_specs=[rtok, vec, vec, vec, rtok, rtok, rtok,
                  _resident_layer((d, HG_IN), idx), _resident((n_lb, d)),
                  _resident((1, HG_DK)), _resident_layer((d, d), idx), _resident((1, d)),
                  _resident((1, d)), seg, state],
        out_specs=[rtok, state],
        out_shape=[tok_shape, state_shape],
        scratch_shapes=scan_scratch + [pltpu.VMEM((tb, d), F32), pltpu.VMEM((tb, d), BF16)],
        compiler_params=_params("arbitrary", "arbitrary"),
        name="hgrn2_backward_scan_readout",
    )(h, mods["sh1"], mods["sc1"], mods["ga1"], q, v, o_f, w["w_in"], w["lb_b"],
      w["norm_w"], w["w_out"], w["ln_g"], w["ln_b"], _segment_matrix(True), s0_b)
    return h_new, s_f, s_b


def _token_block(l):
    for tb in (512, 256, 128):
        if l % tb == 0:
            return tb
    raise ValueError(f"sequence length {l} is not a multiple of 128")


def kernel(x, c, ctx, c_ctx, ada_w, ada_b, ln_g, ln_b, hg_w_in, hg_lb, hg_norm_w, hg_w_out, sgu_w_in, sgu_ln_g, sgu_ln_b, sgu_w_s, sgu_b_s, sgu_w_out, ffn_w_up, ffn_conv_w, ffn_conv_b, ffn_w_down):
    batch, seq, d = x.shape
    ctx_len = ctx.shape[1]
    assert d == D_MODEL and batch + 1 <= ADA_ROWS
    tb_x, tb_c = _token_block(seq), _token_block(ctx_len)
    assert tb_x % GRID_W == 0 and tb_c == ctx_len

    cc = jnp.concatenate([c, c_ctx[None], jnp.zeros((ADA_ROWS - batch - 1, d), F32)], axis=0)
    ada = _ada_modulation(cc, ada_w, ada_b)
    names = ("sh1", "sc1", "ga1", "sh2", "sc2", "ga2")

    def mods_for(layer, is_ctx):
        m = ada[layer].reshape(ADA_ROWS, 6, 1, d)
        if is_ctx:
            rows = jnp.broadcast_to(m[batch:batch + 1], (batch, 6, 1, d))
        else:
            rows = m[:batch]
        return {n: rows[:, i] for i, n in enumerate(names)}

    hg_w_in_b, hg_w_out_b = hg_w_in.astype(BF16), hg_w_out.astype(BF16)
    sgu_w_in_b, sgu_w_out_b = sgu_w_in.astype(BF16), sgu_w_out.astype(BF16)
    ffn_w_up_b, ffn_w_down_b = ffn_w_up.astype(BF16), ffn_w_down.astype(BF16)
    ffn_conv_b3 = ffn_conv_b[:, None, :]
    zero_state = jnp.zeros((batch, HG_HEADS, HG_DK, HG_DK), F32)
    h, hc = x, ctx
    for layer in range(DEPTH):
        kind, idx = layer % N_MIXERS, layer // N_MIXERS
        ctx_later = any(j % N_MIXERS == 0 for j in range(layer + 1, DEPTH))
        mx = mods_for(layer, False)
        mc = mods_for(layer, True) if (kind == 0 or ctx_later) else None
        ln1 = {"ln_g": ln_g[layer, 0][None], "ln_b": ln_b[layer, 0][None]}
        if kind == 0:
            w = dict(ln1, w_in=hg_w_in_b, lb_f=hg_lb[0], lb_b=hg_lb[1],
                     norm_w=hg_norm_w[idx][None], w_out=hg_w_out_b)
            hc_new, s_f, s_b = _hgrn_layer(hc, mc, w, idx, zero_state, zero_state, tb_c)
            h, _, _ = _hgrn_layer(h, mx, w, idx, s_f, s_b, tb_x)
            if ctx_later:
                hc = hc_new
        else:
            w = dict(ln1, w_in=sgu_w_in_b, v_g=sgu_ln_g[idx][None], v_b=sgu_ln_b[idx][None],
                     w_s=sgu_w_s[idx].astype(BF16), b_s=jnp.repeat(sgu_b_s[idx].T, SGU_GW, axis=1),
                     w_out=sgu_w_out_b)
            h = _sgu_layer(h, mx, w, idx, tb_x)
            if ctx_later:
                hc = _sgu_layer(hc, mc, w, idx, tb_c)
        w = dict(w_up=ffn_w_up_b, conv_w=ffn_conv_w, conv_b=ffn_conv_b3, w_down=ffn_w_down_b,
                 ln_g=ln_g[layer, 1][None], ln_b=ln_b[layer, 1][None])
        h = _ffn_layer(h, mx, w, layer, GRID_W, tb_x)
        if ctx_later:
            hc = _ffn_layer(hc, mc, w, layer, ctx_len, tb_c)
    return h
```

```python
import functools

import jax
import jax.numpy as jnp
import numpy as np
from jax import lax
from jax.experimental import pallas as pl
from jax.experimental.pallas import tpu as pltpu

F32 = jnp.float32
BF16 = jnp.bfloat16

LANES = 128
D_MODEL = 1024
DEPTH = 4
N_MIXERS = 2
GRID_W = 64
HG_DK = LANES
HG_HEADS = D_MODEL // HG_DK
HG_COL_Q, HG_COL_GATE, HG_COL_V, HG_COL_F_FWD, HG_COL_F_BWD = (i * D_MODEL for i in range(5))
HG_IN = 5 * D_MODEL
SCAN_CHUNK = 64
SCAN_LEVELS = 6
SUBLANES = 8
SCAN_MM_LEVELS = SCAN_LEVELS - 1
SCAN_SEGMENTS = 3
SCAN_OPERANDS = SCAN_MM_LEVELS + 2
LOG2_E = 1.4426950408889634
SGU_CHUNK = 128
SGU_WIDTH = 3 * D_MODEL
SGU_GROUPS = 8
SGU_GW = SGU_WIDTH // SGU_GROUPS
SGU_TILE = 2 * SGU_GW
D_FF = 128 * ((8 * D_MODEL // 3 + 127) // 128)
FF_TILE = 256
PROJ_TILE = 256
ALPHA = (2.0 * DEPTH) ** 0.25
LN_EPS = 1e-5
RMS_EPS = 1e-6
VMEM_LIMIT_BYTES = 56 * 1024 * 1024
ADA_TILE = 1536
ADA_ROWS = 8

_NT = (((1,), (1,)), ((), ()))
_TN = (((0,), (0,)), ((), ()))


def _params(*sem):
    return pltpu.CompilerParams(dimension_semantics=sem, vmem_limit_bytes=VMEM_LIMIT_BYTES)


def _resident(shape):
    zeros = (0,) * len(shape)
    return pl.BlockSpec(shape, lambda *_: zeros, pipeline_mode=pl.Buffered(1))


def _resident_layer(shape, layer):
    zeros = (0,) * len(shape)
    return pl.BlockSpec((None,) + tuple(shape), lambda *_: (layer,) + zeros,
                        pipeline_mode=pl.Buffered(1))


def _per_batch(shape):
    zeros = (0,) * len(shape)
    return pl.BlockSpec((1,) + tuple(shape), lambda b, j: (b,) + zeros)


def _dot(a, b):
    return jnp.dot(a, b, preferred_element_type=F32)


def _silu(x):
    return x * jax.nn.sigmoid(x)


def _gelu(x):
    return 0.5 * x * (1.0 + lax.erf(x * (2.0 ** -0.5)))


def _layer_norm(x, g, b):
    mu = jnp.mean(x, axis=-1, keepdims=True)
    xc = x - mu
    var = jnp.mean(xc * xc, axis=-1, keepdims=True)
    return xc * lax.rsqrt(var + LN_EPS) * g + b


def _modulated_bf16(h, sh_ref, sc_ref):
    return (h * (1.0 + sc_ref[0]) + sh_ref[0]).astype(BF16)


def _ada_kernel(c_ref, w_ref, b_ref, o_ref):
    s = _silu(c_ref[...])
    o_ref[0] = jnp.dot(s, w_ref[0], precision=lax.Precision.HIGHEST,
                       preferred_element_type=F32) + b_ref[0]


def _ada_modulation(cc, ada_w, ada_b):
    n = ada_w.shape[-1]
    return pl.pallas_call(
        _ada_kernel,
        grid=(DEPTH, n // ADA_TILE),
        in_specs=[pl.BlockSpec((ADA_ROWS, D_MODEL), lambda l, j: (0, 0)),
                  pl.BlockSpec((1, D_MODEL, ADA_TILE), lambda l, j: (l, 0, j)),
                  pl.BlockSpec((1, 1, ADA_TILE), lambda l, j: (l, 0, j))],
        out_specs=pl.BlockSpec((1, ADA_ROWS, ADA_TILE), lambda l, j: (l, 0, j)),
        out_shape=jax.ShapeDtypeStruct((DEPTH, ADA_ROWS, n), F32),
        compiler_params=_params("arbitrary", "arbitrary"),
        name="ada_modulation",
    )(cc, ada_w, ada_b.reshape(DEPTH, 1, n))


def _ffn_kernel(h_ref, sh_ref, sc_ref, ga_ref, wu_ref, cw_ref, cb_ref, wd_ref, lng_ref, lnb_ref,
                o_ref, act_ref, *, row_len):
    h = h_ref[0]
    tb = h.shape[0]
    hm = _modulated_bf16(h, sh_ref, sc_ref)
    col = lax.broadcasted_iota(jnp.int32, (tb, FF_TILE), 0) % row_len
    has_prev = col != 0
    has_next = col != row_len - 1

    def conv(sl):
        y = _dot(hm, wu_ref[:, sl])
        y_prev = jnp.where(has_prev, pltpu.roll(y, 1, 0), 0.0)
        y_next = jnp.where(has_next, pltpu.roll(y, tb - 1, 0), 0.0)
        return (y_prev * cw_ref[0:1, sl] + y * cw_ref[1:2, sl] + y_next * cw_ref[2:3, sl]
                + cb_ref[:, sl])

    for j in range(D_FF // FF_TILE):
        sl = slice(j * FF_TILE, (j + 1) * FF_TILE)
        a = conv(sl)
        g = conv(slice(D_FF + sl.start, D_FF + sl.stop))
        act_ref[:, sl] = (_silu(g) * a).astype(BF16)
    f = _dot(act_ref[...], wd_ref[...])
    o_ref[0] = _layer_norm(ALPHA * h + ga_ref[0] * f, lng_ref[...], lnb_ref[...])


def _ffn_layer(h, mods, w, layer, row_len, tb):
    b, l, d = h.shape
    tok = pl.BlockSpec((1, tb, d), lambda i, j: (i, j, 0))
    vec = _per_batch((1, d))
    return pl.pallas_call(
        functools.partial(_ffn_kernel, row_len=row_len),
        grid=(b, l // tb),
        in_specs=[tok, vec, vec, vec,
                  _resident_layer((d, 2 * D_FF), layer), _resident_layer((3, 2 * D_FF), layer),
                  _resident_layer((1, 2 * D_FF), layer), _resident_layer((D_FF, d), layer),
                  _resident((1, d)), _resident((1, d))],
        out_specs=tok,
        out_shape=jax.ShapeDtypeStruct(h.shape, F32),
        scratch_shapes=[pltpu.VMEM((tb, D_FF), BF16)],
        compiler_params=_params("arbitrary", "arbitrary"),
        name="conv_ffn",
    )(h, mods["sh2"], mods["sc2"], mods["ga2"], w["w_up"], w["conv_w"], w["conv_b"], w["w_down"],
      w["ln_g"], w["ln_b"])


def _sgu_kernel(h_ref, sh_ref, sc_ref, ga_ref, win_ref, vg_ref, vb_ref, ws_ref, bs_ref, wout_ref,
                lng_ref, lnb_ref, o_ref, v_ref, p_ref):
    h = h_ref[0]
    tb = h.shape[0]
    hm = _modulated_bf16(h, sh_ref, sc_ref)
    tiles = [slice(j, j + SGU_TILE) for j in range(0, SGU_WIDTH, SGU_TILE)]

    total = jnp.zeros((tb, 1), F32)
    for sl in tiles:
        z = _gelu(_dot(hm, win_ref[:, SGU_WIDTH + sl.start:SGU_WIDTH + sl.stop]))
        v_ref[:, sl] = z
        total = total + jnp.sum(z, axis=-1, keepdims=True)
    mu = total * (1.0 / SGU_WIDTH)
    sq = jnp.zeros((tb, 1), F32)
    for sl in tiles:
        zc = v_ref[:, sl] - mu
        sq = sq + jnp.sum(zc * zc, axis=-1, keepdims=True)
    rstd = lax.rsqrt(sq * (1.0 / SGU_WIDTH) + LN_EPS)

    for t, sl in enumerate(tiles):
        vn = ((v_ref[:, sl] - mu) * rstd * vg_ref[:, sl] + vb_ref[:, sl]).astype(BF16)
        u = _gelu(_dot(hm, win_ref[:, sl]))
        for gi in range(SGU_TILE // SGU_GW):
            g = t * (SGU_TILE // SGU_GW) + gi
            lo = slice(gi * SGU_GW, (gi + 1) * SGU_GW)
            gl = slice(g * SGU_GW, (g + 1) * SGU_GW)
            for n in range(tb // SGU_CHUNK):
                rows = slice(n * SGU_CHUNK, (n + 1) * SGU_CHUNK)
                mixed = _dot(ws_ref[g], vn[rows, lo]) + bs_ref[:, gl]
                p_ref[rows, gl] = (u[rows, lo] * mixed).astype(BF16)
    y = _dot(p_ref[...], wout_ref[...])
    o_ref[0] = _layer_norm(ALPHA * h + ga_ref[0] * y, lng_ref[...], lnb_ref[...])


def _sgu_layer(h, mods, w, idx, tb):
    b, l, d = h.shape
    tok = pl.BlockSpec((1, tb, d), lambda i, j: (i, j, 0))
    vec = _per_batch((1, d))
    return pl.pallas_call(
        _sgu_kernel,
        grid=(b, l // tb),
        in_specs=[tok, vec, vec, vec,
                  _resident_layer((d, 2 * SGU_WIDTH), idx), _resident((1, SGU_WIDTH)),
                  _resident((1, SGU_WIDTH)),
                  _resident((SGU_GROUPS, SGU_CHUNK, SGU_CHUNK)), _resident((SGU_CHUNK, SGU_WIDTH)),
                  _resident_layer((SGU_WIDTH, d), idx), _resident((1, d)), _resident((1, d))],
        out_specs=tok,
        out_shape=jax.ShapeDtypeStruct(h.shape, F32),
        scratch_shapes=[pltpu.VMEM((tb, SGU_WIDTH), F32), pltpu.VMEM((tb, SGU_WIDTH), BF16)],
        compiler_params=_params("arbitrary", "arbitrary"),
        name="chunk_sgu",
    )(h, mods["sh1"], mods["sc1"], mods["ga1"], w["w_in"], w["v_g"], w["v_b"], w["w_s"], w["b_s"],
      w["w_out"], w["ln_g"], w["ln_b"])


def _lower_bound(lb_ref, idx):
    x = lb_ref[...]
    e = jnp.exp(x - jnp.max(x, axis=0, keepdims=True))
    p = e / jnp.sum(e, axis=0, keepdims=True)
    acc = p[0:1]
    for r in range(1, idx + 1):
        acc = acc + p[r:r + 1]
    return acc - p[0:1]


def _level_half(i):
    return (SCAN_CHUNK // 2) >> i


def _segment_matrix(rev):
    c = SCAN_CHUNK
    t = np.arange(c)[:, None]
    r = np.arange(c)[None, :]
    blocks = [r >= t] if rev else [r <= t]
    for i in range(SCAN_MM_LEVELS):
        m = _level_half(i)
        if m >= SUBLANES:
            continue
        upper = (t % (2 * m)) >= m
        if rev:
            ref = t - t % (2 * m) + m
            blocks.append(np.where(upper, (r >= ref) & (r < t), (r >= t) & (r < ref)))
        else:
            ref = t - t % (2 * m) + m - 1
            blocks.append(np.where(upper, (r > ref) & (r <= t), (r > t) & (r <= ref)))
    assert len(blocks) == SCAN_SEGMENTS
    u = np.concatenate(blocks, axis=0).astype(np.float32)
    return jnp.asarray(np.tile(u, (1, 3)), BF16)


def _pair_owner(rev):
    c = SCAN_CHUNK
    t = lax.broadcasted_iota(jnp.int32, (c, c), 0)
    s = lax.broadcasted_iota(jnp.int32, (c, c), 1)
    x = t ^ s
    owner = jnp.where(t == s, SCAN_LEVELS, -1)
    for i in range(SCAN_LEVELS):
        m = (c // 2) >> i
        owner = jnp.where((x >= m) & (x < 2 * m), i, owner)
    return jnp.where((t <= s) if rev else (t >= s), owner, -1)


def _query_rows(rev):
    row = lax.broadcasted_iota(jnp.int32, (SCAN_CHUNK, D_MODEL), 0)
    masks = []
    for i in range(SCAN_MM_LEVELS):
        upper = (row & ((SCAN_CHUNK // 2) >> i)) != 0
        masks.append(jnp.logical_not(upper) if rev else upper)
    return masks


def _shift(sl, offset):
    return slice(sl.start + offset, sl.stop + offset)


def _column_tiles(width):
    return [slice(j, j + PROJ_TILE) for j in range(0, width, PROJ_TILE)]


def _forget_gate(z, lb, f_ref, lg3_ref, sl):
    f = lb + (1.0 - lb) * jax.nn.sigmoid(z)
    f_ref[:, sl] = f
    lg = jnp.log(f) * LOG2_E
    hi = lg.astype(BF16)
    rest = lg - hi.astype(F32)
    mid = rest.astype(BF16)
    lg3_ref[0, :, sl] = hi
    lg3_ref[1, :, sl] = mid
    lg3_ref[2, :, sl] = (rest - mid.astype(F32)).astype(BF16)


def _slab_level(gsum, q, k, m, rev):
    parts = []
    for start in range(0, SCAN_CHUNK, 2 * m):
        lo, hi = slice(start, start + m), slice(start + m, start + 2 * m)
        if rev:
            ref = gsum[start + m:start + m + 1]
            parts += [q[lo] * jnp.exp2(gsum[lo] - ref), k[hi] * jnp.exp2(ref - gsum[hi])]
        else:
            ref = gsum[start + m - 1:start + m]
            parts += [k[lo] * jnp.exp2(ref - gsum[lo]), q[hi] * jnp.exp2(gsum[hi] - ref)]
    return jnp.concatenate(parts, axis=0)


def _scan_operands(ci, q_ref, f_ref, lg3_ref, u_ref, x_ref, dec_ref, diag_ref, rev):
    c = SCAN_CHUNK
    query_rows = _query_rows(rev)
    lane = lax.broadcasted_iota(jnp.int32, (c, LANES), 1)
    rows = _chunk_rows(ci)
    lg3 = jnp.concatenate([lg3_ref[0, rows, :], lg3_ref[1, rows, :], lg3_ref[2, rows, :]], axis=0)
    seg = _dot(u_ref[...], lg3)
    gsum = seg[0:c]
    q = q_ref[0, rows, :]
    f = f_ref[rows, :]
    k = 1.0 - f
    block = 1
    for i in range(SCAN_MM_LEVELS):
        if _level_half(i) >= SUBLANES:
            x = _slab_level(gsum, q, k, _level_half(i), rev)
        else:
            x = jnp.where(query_rows[i], q, k) * jnp.exp2(seg[block * c:(block + 1) * c])
            block += 1
        x_ref[i, rows, :] = x.astype(BF16)
    g_end = gsum[0:1] if rev else gsum[c - 1:c]
    x_ref[SCAN_MM_LEVELS, rows, :] = (q * jnp.exp2(gsum)).astype(BF16)
    x_ref[SCAN_MM_LEVELS + 1, rows, :] = (k * jnp.exp2(g_end - gsum)).astype(BF16)
    dec_ref[ci] = jnp.exp2(g_end)
    k_before = pltpu.roll(k, c - 1 if rev else 1, 0)
    diag = jnp.zeros((c, LANES), F32)
    for j, prod in enumerate((q * k, q * f * k_before)):
        for hd in range(HG_HEADS):
            head_sum = jnp.sum(prod[:, hd * HG_DK:(hd + 1) * HG_DK], axis=-1, keepdims=True)
            diag = jnp.where(lane == j * HG_HEADS + hd, head_sum, diag)
    diag_ref[rows, :] = diag


def _chunk_rows(ci):
    return slice(ci * SCAN_CHUNK, (ci + 1) * SCAN_CHUNK)


def _scan_block(n_chunks, rev, operands, step):
    order = list(range(n_chunks))[::-1] if rev else list(range(n_chunks))
    operands(order[0])
    for i, ci in enumerate(order):
        if i + 1 < n_chunks:
            operands(order[i + 1])
        step(ci)


def _chunk_scores(x_ref, diag_ref, rows, owner):
    diag = diag_ref[rows, :]
    scores = []
    for hd in range(HG_HEADS):
        cols = slice(hd * HG_DK, (hd + 1) * HG_DK)
        s = jnp.where(owner == SCAN_LEVELS, diag[:, hd:hd + 1],
                      jnp.where(owner == SCAN_MM_LEVELS,
                                diag[:, HG_HEADS + hd:HG_HEADS + hd + 1], 0.0))
        for i in range(SCAN_MM_LEVELS):
            x = x_ref[i, rows, cols]
            s = jnp.where(owner == i, lax.dot_general(x, x, _NT, preferred_element_type=F32), s)
        scores.append(s.astype(BF16))
    return scores


def _scan_step(x_ref, dec, rows, cols, scores, vb, st):
    o = _dot(scores, vb) + _dot(x_ref[SCAN_MM_LEVELS, rows, cols], st.astype(BF16))
    dec_rows = jnp.broadcast_to(dec, (HG_DK, HG_DK)).T
    st_new = st * dec_rows + lax.dot_general(x_ref[SCAN_MM_LEVELS + 1, rows, cols], vb, _TN,
                                             preferred_element_type=F32)
    return o, st_new


def _hg_fwd_kernel(h_ref, sh_ref, sc_ref, win_ref, lb_ref, u_ref, s0_ref,
                   o_ref, q_ref, v_ref, st_ref, f_scr, lg3_scr, x_scr, dec_scr, diag_scr, *, lb_idx):
    @pl.when(pl.program_id(1) == 0)
    def _():
        st_ref[...] = s0_ref[...]

    h = h_ref[0]
    tb = h.shape[0]
    hm = _modulated_bf16(h, sh_ref, sc_ref)
    lb = _lower_bound(lb_ref, lb_idx)
    for sl in _column_tiles(h.shape[1]):
        _forget_gate(_dot(hm, win_ref[:, _shift(sl, HG_COL_F_FWD)]), lb[:, sl], f_scr, lg3_scr, sl)
        q_ref[0, :, sl] = _silu(_dot(hm, win_ref[:, _shift(sl, HG_COL_Q)]))
        v_ref[0, :, sl] = _dot(hm, win_ref[:, _shift(sl, HG_COL_V)]).astype(BF16)
    owner = _pair_owner(False)

    def operands(ci):
        _scan_operands(ci, q_ref, f_scr, lg3_scr, u_ref, x_scr, dec_scr, diag_scr, False)

    def step(ci):
        rows = _chunk_rows(ci)
        dec = dec_scr[ci]
        scores = _chunk_scores(x_scr, diag_scr, rows, owner)
        for hd in range(HG_HEADS):
            cols = slice(hd * HG_DK, (hd + 1) * HG_DK)
            o, st = _scan_step(x_scr, dec[:, cols], rows, cols, scores[hd],
                               v_ref[0, rows, cols], st_ref[0, hd])
            o_ref[0, rows, cols] = o
            st_ref[0, hd] = st

    _scan_block(tb // SCAN_CHUNK, False, operands, step)


def _hg_bwd_kernel(h_ref, sh_ref, sc_ref, ga_ref, q_ref, v_ref, of_ref, win_ref, lb_ref,
                   nw_ref, wo_ref, lng_ref, lnb_ref, u_ref, s0_ref,
                   hn_ref, st_ref, f_scr, lg3_scr, x_scr, dec_scr, diag_scr, g_scr, y_scr, *, lb_idx):
    @pl.when(pl.program_id(1) == 0)
    def _():
        st_ref[...] = s0_ref[...]

    h = h_ref[0]
    tb = h.shape[0]
    hm = _modulated_bf16(h, sh_ref, sc_ref)
    lb = _lower_bound(lb_ref, lb_idx)
    for sl in _column_tiles(h.shape[1]):
        _forget_gate(_dot(hm, win_ref[:, _shift(sl, HG_COL_F_BWD)]), lb[:, sl], f_scr, lg3_scr, sl)
        g_scr[:, sl] = _silu(_dot(hm, win_ref[:, _shift(sl, HG_COL_GATE)]))
    owner = _pair_owner(True)
    norm_w = nw_ref[...]

    def operands(ci):
        _scan_operands(ci, q_ref, f_scr, lg3_scr, u_ref, x_scr, dec_scr, diag_scr, True)

    def step(ci):
        rows = _chunk_rows(ci)
        dec = dec_scr[ci]
        scores = _chunk_scores(x_scr, diag_scr, rows, owner)
        for hd in range(HG_HEADS):
            cols = slice(hd * HG_DK, (hd + 1) * HG_DK)
            o, st = _scan_step(x_scr, dec[:, cols], rows, cols, scores[hd],
                               v_ref[0, rows, cols], st_ref[0, hd])
            st_ref[0, hd] = st
            o = o + of_ref[0, rows, cols]
            o = o * lax.rsqrt(jnp.mean(o * o, axis=-1, keepdims=True) + RMS_EPS) * norm_w
            y_scr[rows, cols] = (o * g_scr[rows, cols]).astype(BF16)

    _scan_block(tb // SCAN_CHUNK, True, operands, step)
    y = _dot(y_scr[...], wo_ref[...])
    hn_ref[0] = _layer_norm(ALPHA * h + ga_ref[0] * y, lng_ref[...], lnb_ref[...])


def _hgrn_layer(h, mods, w, idx, s0_f, s0_b, tb):
    b, l, d = h.shape
    nb = l // tb
    vec = _per_batch((1, d))
    state = _per_batch((HG_HEADS, HG_DK, HG_DK))
    state_shape = jax.ShapeDtypeStruct((b, HG_HEADS, HG_DK, HG_DK), F32)
    tok_shape = jax.ShapeDtypeStruct(h.shape, F32)
    n_lb = w["lb_f"].shape[0]
    seg = _resident((SCAN_SEGMENTS * SCAN_CHUNK, 3 * SCAN_CHUNK))
    scan_scratch = [pltpu.VMEM((tb, d), F32),
                    pltpu.VMEM((3, tb, d), BF16),
                    pltpu.VMEM((SCAN_OPERANDS, tb, d), BF16),
                    pltpu.VMEM((tb // SCAN_CHUNK, 1, d), F32),
                    pltpu.VMEM((tb, LANES), F32)]

    tok = pl.BlockSpec((1, tb, d), lambda i, j: (i, j, 0))
    o_f, q, v, s_f = pl.pallas_call(
        functools.partial(_hg_fwd_kernel, lb_idx=idx),
        grid=(b, nb),
        in_specs=[tok, vec, vec, _resident_layer((d, HG_IN), idx), _resident((n_lb, d)), seg, state],
        out_specs=[tok, tok, tok, state],
        out_shape=[tok_shape, tok_shape, jax.ShapeDtypeStruct(h.shape, BF16), state_shape],
        scratch_shapes=scan_scratch,
        compiler_params=_params("arbitrary", "arbitrary"),
        name="hgrn2_forward_scan",
    )(h, mods["sh1"], mods["sc1"], w["w_in"], w["lb_f"], _segment_matrix(False), s0_f)

    rtok = pl.BlockSpec((1, tb, d), lambda i, j: (i, nb - 1 - j, 0))
    h_new, s_b = pl.pallas_call(
        functools.partial(_hg_bwd_kernel, lb_idx=idx),
        grid=(b, nb),
        in_specs=[rtok, vec, vec, vec, rtok, rtok, rtok,
                  _resident_layer((d, HG_IN), idx), _resident((n_lb, d)),
                  _resident((1, HG_DK)), _resident_layer((d, d), idx), _resident((1, d)),
                  _resident((1, d)), seg, state],
        out_specs=[rtok, state],
        out_shape=[tok_shape, state_shape],
        scratch_shapes=scan_scratch + [pltpu.VMEM((tb, d), F32), pltpu.VMEM((tb, d), BF16)],
        compiler_params=_params("arbitrary", "arbitrary"),
        name="hgrn2_backward_scan_readout",
    )(h, mods["sh1"], mods["sc1"], mods["ga1"], q, v, o_f, w["w_in"], w["lb_b"],
      w["norm_w"], w["w_out"], w["ln_g"], w["ln_b"], _segment_matrix(True), s0_b)
    return h_new, s_f, s_b


def _token_block(l):
    for tb in (512, 256, 128):
        if l % tb == 0:
            return tb
    raise ValueError(f"sequence length {l} is not a multiple of 128")


def kernel(x, c, ctx, c_ctx, ada_w, ada_b, ln_g, ln_b, hg_w_in, hg_lb, hg_norm_w, hg_w_out, sgu_w_in, sgu_ln_g, sgu_ln_b, sgu_w_s, sgu_b_s, sgu_w_out, ffn_w_up, ffn_conv_w, ffn_conv_b, ffn_w_down):
    batch, seq, d = x.shape
    ctx_len = ctx.shape[1]
    assert d == D_MODEL and batch + 1 <= ADA_ROWS
    tb_x, tb_c = _token_block(seq), _token_block(ctx_len)
    assert tb_x % GRID_W == 0 and tb_c == ctx_len

    cc = jnp.concatenate([c, c_ctx[None], jnp.zeros((ADA_ROWS - batch - 1, d), F32)], axis=0)
    ada = _ada_modulation(cc, ada_w, ada_b)
    names = ("sh1", "sc1", "ga1", "sh2", "sc2", "ga2")

    def mods_for(layer, is_ctx):
        m = ada[layer].reshape(ADA_ROWS, 6, 1, d)
        if is_ctx:
            rows = jnp.broadcast_to(m[batch:batch + 1], (batch, 6, 1, d))
        else:
            rows = m[:batch]
        return {n: rows[:, i] for i, n in enumerate(names)}

    hg_w_in_b, hg_w_out_b = hg_w_in.astype(BF16), hg_w_out.astype(BF16)
    sgu_w_in_b, sgu_w_out_b = sgu_w_in.astype(BF16), sgu_w_out.astype(BF16)
    ffn_w_up_b, ffn_w_down_b = ffn_w_up.astype(BF16), ffn_w_down.astype(BF16)
    ffn_conv_b3 = ffn_conv_b[:, None, :]
    zero_state = jnp.zeros((batch, HG_HEADS, HG_DK, HG_DK), F32)
    h, hc = x, ctx
    for layer in range(DEPTH):
        kind, idx = layer % N_MIXERS, layer // N_MIXERS
        ctx_later = any(j % N_MIXERS == 0 for j in range(layer + 1, DEPTH))
        mx = mods_for(layer, False)
        mc = mods_for(layer, True) if (kind == 0 or ctx_later) else None
        ln1 = {"ln_g": ln_g[layer, 0][None], "ln_b": ln_b[layer, 0][None]}
        if kind == 0:
            w = dict(ln1, w_in=hg_w_in_b, lb_f=hg_lb[0], lb_b=hg_lb[1],
                     norm_w=hg_norm_w[idx][None], w_out=hg_w_out_b)
            hc_new, s_f, s_b = _hgrn_layer(hc, mc, w, idx, zero_state, zero_state, tb_c)
            h, _, _ = _hgrn_layer(h, mx, w, idx, s_f, s_b, tb_x)
            if ctx_later:
                hc = hc_new
        else:
            w = dict(ln1, w_in=sgu_w_in_b, v_g=sgu_ln_g[idx][None], v_b=sgu_ln_b[idx][None],
                     w_s=sgu_w_s[idx].astype(BF16), b_s=jnp.repeat(sgu_b_s[idx].T, SGU_GW, axis=1),
                     w_out=sgu_w_out_b)
            h = _sgu_layer(h, mx, w, idx, tb_x)
            if ctx_later:
                hc = _sgu_layer(hc, mc, w, idx, tb_c)
        w = dict(w_up=ffn_w_up_b, conv_w=ffn_conv_w, conv_b=ffn_conv_b3, w_down=ffn_w_down_b,
                 ln_g=ln_g[layer, 1][None], ln_b=ln_b[layer, 1][None])
        h = _ffn_layer(h, mx, w, layer, GRID_W, tb_x)
        if ctx_later:
            hc = _ffn_layer(hc, mc, w, layer, ctx_len, tb_c)
    return h
```

```python
import functools

import jax
import jax.numpy as jnp
import numpy as np
from jax import lax
from jax.experimental import pallas as pl
from jax.experimental.pallas import tpu as pltpu

F32 = jnp.float32
BF16 = jnp.bfloat16

LANES = 128
D_MODEL = 1024
DEPTH = 4
N_MIXERS = 2
GRID_W = 64
HG_DK = LANES
HG_HEADS = D_MODEL // HG_DK
HG_COL_Q, HG_COL_GATE, HG_COL_V, HG_COL_F_FWD, HG_COL_F_BWD = (i * D_MODEL for i in range(5))
HG_IN = 5 * D_MODEL
SCAN_CHUNK = 64
SCAN_LEVELS = 6
SUBLANES = 8
SCAN_MM_LEVELS = SCAN_LEVELS - 1
SCAN_SEGMENTS = 3
SCAN_OPERANDS = SCAN_MM_LEVELS + 2
LOG2_E = 1.4426950408889634
SGU_CHUNK = 128
SGU_WIDTH = 3 * D_MODEL
SGU_GROUPS = 8
SGU_GW = SGU_WIDTH // SGU_GROUPS
SGU_TILE = 2 * SGU_GW
D_FF = 128 * ((8 * D_MODEL // 3 + 127) // 128)
FF_TILE = 256
PROJ_TILE = 256
ALPHA = (2.0 * DEPTH) ** 0.25
LN_EPS = 1e-5
RMS_EPS = 1e-6
VMEM_LIMIT_BYTES = 56 * 1024 * 1024
ADA_TILE = 1536
ADA_ROWS = 8

_NT = (((1,), (1,)), ((), ()))
_TN = (((0,), (0,)), ((), ()))


def _params(*sem):
    return pltpu.CompilerParams(dimension_semantics=sem, vmem_limit_bytes=VMEM_LIMIT_BYTES)


def _resident(shape):
    zeros = (0,) * len(shape)
    return pl.BlockSpec(shape, lambda *_: zeros, pipeline_mode=pl.Buffered(1))


def _resident_layer(shape, layer):
    zeros = (0,) * len(shape)
    return pl.BlockSpec((None,) + tuple(shape), lambda *_: (layer,) + zeros,
                        pipeline_mode=pl.Buffered(1))


def _per_batch(shape):
    zeros = (0,) * len(shape)
    return pl.BlockSpec((1,) + tuple(shape), lambda b, j: (b,) + zeros)


def _dot(a, b):
    return jnp.dot(a, b, preferred_element_type=F32)


def _silu(x):
    return x * jax.nn.sigmoid(x)


def _gelu(x):
    return 0.5 * x * (1.0 + lax.erf(x * (2.0 ** -0.5)))


def _layer_norm(x, g, b):
    mu = jnp.mean(x, axis=-1, keepdims=True)
    xc = x - mu
    var = jnp.mean(xc * xc, axis=-1, keepdims=True)
    return xc * lax.rsqrt(var + LN_EPS) * g + b


def _modulated_bf16(h, sh_ref, sc_ref):
    return (h * (1.0 + sc_ref[0]) + sh_ref[0]).astype(BF16)


def _ada_kernel(c_ref, w_ref, b_ref, o_ref):
    s = _silu(c_ref[...])
    o_ref[0] = jnp.dot(s, w_ref[0], precision=lax.Precision.HIGHEST,
                       preferred_element_type=F32) + b_ref[0]


def _ada_modulation(cc, ada_w, ada_b):
    n = ada_w.shape[-1]
    return pl.pallas_call(
        _ada_kernel,
        grid=(DEPTH, n // ADA_TILE),
        in_specs=[pl.BlockSpec((ADA_ROWS, D_MODEL), lambda l, j: (0, 0)),
                  pl.BlockSpec((1, D_MODEL, ADA_TILE), lambda l, j: (l, 0, j)),
                  pl.BlockSpec((1, 1, ADA_TILE), lambda l, j: (l, 0, j))],
        out_specs=pl.BlockSpec((1, ADA_ROWS, ADA_TILE), lambda l, j: (l, 0, j)),
        out_shape=jax.ShapeDtypeStruct((DEPTH, ADA_ROWS, n), F32),
        compiler_params=_params("arbitrary", "arbitrary"),
        name="ada_modulation",
    )(cc, ada_w, ada_b.reshape(DEPTH, 1, n))


def _ffn_kernel(h_ref, sh_ref, sc_ref, ga_ref, wu_ref, cw_ref, cb_ref, wd_ref, lng_ref, lnb_ref,
                o_ref, act_ref, *, row_len):
    h = h_ref[0]
    tb = h.shape[0]
    hm = _modulated_bf16(h, sh_ref, sc_ref)
    col = lax.broadcasted_iota(jnp.int32, (tb, FF_TILE), 0) % row_len
    has_prev = col != 0
    has_next = col != row_len - 1

    def conv(sl):
        y = _dot(hm, wu_ref[:, sl])
        y_prev = jnp.where(has_prev, pltpu.roll(y, 1, 0), 0.0)
        y_next = jnp.where(has_next, pltpu.roll(y, tb - 1, 0), 0.0)
        return (y_prev * cw_ref[0:1, sl] + y * cw_ref[1:2, sl] + y_next * cw_ref[2:3, sl]
                + cb_ref[:, sl])

    for j in range(D_FF // FF_TILE):
        sl = slice(j * FF_TILE, (j + 1) * FF_TILE)
        a = conv(sl)
        g = conv(slice(D_FF + sl.start, D_FF + sl.stop))
        act_ref[:, sl] = (_silu(g) * a).astype(BF16)
    f = _dot(act_ref[...], wd_ref[...])
    o_ref[0] = _layer_norm(ALPHA * h + ga_ref[0] * f, lng_ref[...], lnb_ref[...])


def _ffn_layer(h, mods, w, layer, row_len, tb):
    b, l, d = h.shape
    tok = pl.BlockSpec((1, tb, d), lambda i, j: (i, j, 0))
    vec = _per_batch((1, d))
    return pl.pallas_call(
        functools.partial(_ffn_kernel, row_len=row_len),
        grid=(b, l // tb),
        in_specs=[tok, vec, vec, vec,
                  _resident_layer((d, 2 * D_FF), layer), _resident_layer((3, 2 * D_FF), layer),
                  _resident_layer((1, 2 * D_FF), layer), _resident_layer((D_FF, d), layer),
                  _resident((1, d)), _resident((1, d))],
        out_specs=tok,
        out_shape=jax.ShapeDtypeStruct(h.shape, F32),
        scratch_shapes=[pltpu.VMEM((tb, D_FF), BF16)],
        compiler_params=_params("arbitrary", "arbitrary"),
        name="conv_ffn",
    )(h, mods["sh2"], mods["sc2"], mods["ga2"], w["w_up"], w["conv_w"], w["conv_b"], w["w_down"],
      w["ln_g"], w["ln_b"])


def _sgu_kernel(h_ref, sh_ref, sc_ref, ga_ref, win_ref, vg_ref, vb_ref, ws_ref, bs_ref, wout_ref,
                lng_ref, lnb_ref, o_ref, v_ref, p_ref):
    h = h_ref[0]
    tb = h.shape[0]
    hm = _modulated_bf16(h, sh_ref, sc_ref)
    tiles = [slice(j, j + SGU_TILE) for j in range(0, SGU_WIDTH, SGU_TILE)]

    total = jnp.zeros((tb, 1), F32)
    for sl in tiles:
        z = _gelu(_dot(hm, win_ref[:, SGU_WIDTH + sl.start:SGU_WIDTH + sl.stop]))
        v_ref[:, sl] = z
        total = total + jnp.sum(z, axis=-1, keepdims=True)
    mu = total * (1.0 / SGU_WIDTH)
    sq = jnp.zeros((tb, 1), F32)
    for sl in tiles:
        zc = v_ref[:, sl] - mu
        sq = sq + jnp.sum(zc * zc, axis=-1, keepdims=True)
    rstd = lax.rsqrt(sq * (1.0 / SGU_WIDTH) + LN_EPS)

    for t, sl in enumerate(tiles):
        vn = ((v_ref[:, sl] - mu) * rstd * vg_ref[:, sl] + vb_ref[:, sl]).astype(BF16)
        u = _gelu(_dot(hm, win_ref[:, sl]))
        for gi in range(SGU_TILE // SGU_GW):
            g = t * (SGU_TILE // SGU_GW) + gi
            lo = slice(gi * SGU_GW, (gi + 1) * SGU_GW)
            gl = slice(g * SGU_GW, (g + 1) * SGU_GW)
            for n in range(tb // SGU_CHUNK):
                rows = slice(n * SGU_CHUNK, (n + 1) * SGU_CHUNK)
                mixed = _dot(ws_ref[g], vn[rows, lo]) + bs_ref[:, gl]
                p_ref[rows, gl] = (u[rows, lo] * mixed).astype(BF16)
    y = _dot(p_ref[...], wout_ref[...])
    o_ref[0] = _layer_norm(ALPHA * h + ga_ref[0] * y, lng_ref[...], lnb_ref[...])


def _sgu_layer(h, mods, w, idx, tb):
    b, l, d = h.shape
    tok = pl.BlockSpec((1, tb, d), lambda i, j: (i, j, 0))
    vec = _per_batch((1, d))
    return pl.pallas_call(
        _sgu_kernel,
        grid=(b, l // tb),
        in_specs=[tok, vec, vec, vec,
                  _resident_layer((d, 2 * SGU_WIDTH), idx), _resident((1, SGU_WIDTH)),
                  _resident((1, SGU_WIDTH)),
                  _resident((SGU_GROUPS, SGU_CHUNK, SGU_CHUNK)), _resident((SGU_CHUNK, SGU_WIDTH)),
                  _resident_layer((SGU_WIDTH, d), idx), _resident((1, d)), _resident((1, d))],
        out_specs=tok,
        out_shape=jax.ShapeDtypeStruct(h.shape, F32),
        scratch_shapes=[pltpu.VMEM((tb, SGU_WIDTH), F32), pltpu.VMEM((tb, SGU_WIDTH), BF16)],
        compiler_params=_params("arbitrary", "arbitrary"),
        name="chunk_sgu",
    )(h, mods["sh1"], mods["sc1"], mods["ga1"], w["w_in"], w["v_g"], w["v_b"], w["w_s"], w["b_s"],
      w["w_out"], w["ln_g"], w["ln_b"])


def _lower_bound(lb_ref, idx):
    x = lb_ref[...]
    e = jnp.exp(x - jnp.max(x, axis=0, keepdims=True))
    p = e / jnp.sum(e, axis=0, keepdims=True)
    acc = p[0:1]
    for r in range(1, idx + 1):
        acc = acc + p[r:r + 1]
    return acc - p[0:1]


def _level_half(i):
    return (SCAN_CHUNK // 2) >> i


def _segment_matrix(rev):
    c = SCAN_CHUNK
    t = np.arange(c)[:, None]
    r = np.arange(c)[None, :]
    blocks = [r >= t] if rev else [r <= t]
    for i in range(SCAN_MM_LEVELS):
        m = _level_half(i)
        if m >= SUBLANES:
            continue
        upper = (t % (2 * m)) >= m
        if rev:
            ref = t - t % (2 * m) + m
            blocks.append(np.where(upper, (r >= ref) & (r < t), (r >= t) & (r < ref)))
        else:
            ref = t - t % (2 * m) + m - 1
            blocks.append(np.where(upper, (r > ref) & (r <= t), (r > t) & (r <= ref)))
    assert len(blocks) == SCAN_SEGMENTS
    u = np.concatenate(blocks, axis=0).astype(np.float32)
    return jnp.asarray(np.tile(u, (1, 3)), BF16)


def _pair_owner(rev):
    c = SCAN_CHUNK
    t = lax.broadcasted_iota(jnp.int32, (c, c), 0)
    s = lax.broadcasted_iota(jnp.int32, (c, c), 1)
    x = t ^ s
    owner = jnp.where(t == s, SCAN_LEVELS, -1)
    for i in range(SCAN_LEVELS):
        m = (c // 2) >> i
        owner = jnp.where((x >= m) & (x < 2 * m), i, owner)
    return jnp.where((t <= s) if rev else (t >= s), owner, -1)


def _query_rows(rev):
    row = lax.broadcasted_iota(jnp.int32, (SCAN_CHUNK, D_MODEL), 0)
    masks = []
    for i in range(SCAN_MM_LEVELS):
        upper = (row & ((SCAN_CHUNK // 2) >> i)) != 0
        masks.append(jnp.logical_not(upper) if rev else upper)
    return masks


def _shift(sl, offset):
    return slice(sl.start + offset, sl.stop + offset)


def _column_tiles(width):
    return [slice(j, j + PROJ_TILE) for j in range(0, width, PROJ_TILE)]


def _forget_gate(z, lb, f_ref, lg3_ref, sl):
    f = lb + (1.0 - lb) * jax.nn.sigmoid(z)
    f_ref[:, sl] = f
    lg = jnp.log(f) * LOG2_E
    hi = lg.astype(BF16)
    rest = lg - hi.astype(F32)
    mid = rest.astype(BF16)
    lg3_ref[0, :, sl] = hi
    lg3_ref[1, :, sl] = mid
    lg3_ref[2, :, sl] = (rest - mid.astype(F32)).astype(BF16)


def _slab_level(gsum, q, k, m, rev):
    parts = []
    for start in range(0, SCAN_CHUNK, 2 * m):
        lo, hi = slice(start, start + m), slice(start + m, start + 2 * m)
        if rev:
            ref = gsum[start + m:start + m + 1]
            parts += [q[lo] * jnp.exp2(gsum[lo] - ref), k[hi] * jnp.exp2(ref - gsum[hi])]
        else:
            ref = gsum[start + m - 1:start + m]
            parts += [k[lo] * jnp.exp2(ref - gsum[lo]), q[hi] * jnp.exp2(gsum[hi] - ref)]
    return jnp.concatenate(parts, axis=0)


def _scan_operands(ci, q_ref, f_ref, lg3_ref, u_ref, x_ref, dec_ref, diag_ref, rev):
    c = SCAN_CHUNK
    query_rows = _query_rows(rev)
    lane = lax.broadcasted_iota(jnp.int32, (c, LANES), 1)
    rows = _chunk_rows(ci)
    lg3 = jnp.concatenate([lg3_ref[0, rows, :], lg3_ref[1, rows, :], lg3_ref[2, rows, :]], axis=0)
    seg = _dot(u_ref[...], lg3)
    gsum = seg[0:c]
    q = q_ref[0, rows, :]
    f = f_ref[rows, :]
    k = 1.0 - f
    block = 1
    for i in range(SCAN_MM_LEVELS):
        if _level_half(i) >= SUBLANES:
            x = _slab_level(gsum, q, k, _level_half(i), rev)
        else:
            x = jnp.where(query_rows[i], q, k) * jnp.exp2(seg[block * c:(block + 1) * c])
            block += 1
        x_ref[i, rows, :] = x.astype(BF16)
    g_end = gsum[0:1] if rev else gsum[c - 1:c]
    x_ref[SCAN_MM_LEVELS, rows, :] = (q * jnp.exp2(gsum)).astype(BF16)
    x_ref[SCAN_MM_LEVELS + 1, rows, :] = (k * jnp.exp2(g_end - gsum)).astype(BF16)
    dec_ref[ci] = jnp.exp2(g_end)
    k_before = pltpu.roll(k, c - 1 if rev else 1, 0)
    diag = jnp.zeros((c, LANES), F32)
    for j, prod in enumerate((q * k, q * f * k_before)):
        for hd in range(HG_HEADS):
            head_sum = jnp.sum(prod[:, hd * HG_DK:(hd + 1) * HG_DK], axis=-1, keepdims=True)
            diag = jnp.where(lane == j * HG_HEADS + hd, head_sum, diag)
    diag_ref[rows, :] = diag


def _chunk_rows(ci):
    return slice(ci * SCAN_CHUNK, (ci + 1) * SCAN_CHUNK)


def _scan_block(n_chunks, rev, operands, step):
    order = list(range(n_chunks))[::-1] if rev else list(range(n_chunks))
    operands(order[0])
    for i, ci in enumerate(order):
        if i + 1 < n_chunks:
            operands(order[i + 1])
        step(ci)


def _chunk_scores(x_ref, diag_ref, rows, owner):
    diag = diag_ref[rows, :]
    scores = []
    for hd in range(HG_HEADS):
        cols = slice(hd * HG_DK, (hd + 1) * HG_DK)
        s = jnp.where(owner == SCAN_LEVELS, diag[:, hd:hd + 1],
                      jnp.where(owner == SCAN_MM_LEVELS,
                                diag[:, HG_HEADS + hd:HG_HEADS + hd + 1], 0.0))
        for i in range(SCAN_MM_LEVELS):
            x = x_ref[i, rows, cols]
            s = jnp.where(owner == i, lax.dot_general(x, x, _NT, preferred_element_type=F32), s)
        scores.append(s.astype(BF16))
    return scores


def _scan_step(x_ref, dec, rows, cols, scores, vb, st):
    o = _dot(scores, vb) + _dot(x_ref[SCAN_MM_LEVELS, rows, cols], st.astype(BF16))
    dec_rows = jnp.broadcast_to(dec, (HG_DK, HG_DK)).T
    st_new = st * dec_rows + lax.dot_general(x_ref[SCAN_MM_LEVELS + 1, rows, cols], vb, _TN,
                                             preferred_element_type=F32)
    return o, st_new


def _hg_fwd_kernel(h_ref, sh_ref, sc_ref, win_ref, lb_ref, u_ref, s0_ref,
                   o_ref, q_ref, v_ref, st_ref, f_scr, lg3_scr, x_scr, dec_scr, diag_scr, q_scr, v_scr,
                   *, lb_idx):
    @pl.when(pl.program_id(1) == 0)
    def _():
        st_ref[...] = s0_ref[...]

    h = h_ref[0]
    tb = h.shape[0]
    hm = _modulated_bf16(h, sh_ref, sc_ref)
    lb = _lower_bound(lb_ref, lb_idx)
    for sl in _column_tiles(h.shape[1]):
        _forget_gate(_dot(hm, win_ref[:, _shift(sl, HG_COL_F_FWD)]), lb[:, sl], f_scr, lg3_scr, sl)
        q = _silu(_dot(hm, win_ref[:, _shift(sl, HG_COL_Q)]))
        q_ref[0, :, sl] = q
        q_scr[0, :, sl] = q
        v = _dot(hm, win_ref[:, _shift(sl, HG_COL_V)]).astype(BF16)
        v_ref[0, :, sl] = v
        v_scr[:, sl] = v
    owner = _pair_owner(False)

    def operands(ci):
        _scan_operands(ci, q_scr, f_scr, lg3_scr, u_ref, x_scr, dec_scr, diag_scr, False)

    def step(ci):
        rows = _chunk_rows(ci)
        dec = dec_scr[ci]
        scores = _chunk_scores(x_scr, diag_scr, rows, owner)
        for hd in range(HG_HEADS):
            cols = slice(hd * HG_DK, (hd + 1) * HG_DK)
            o, st = _scan_step(x_scr, dec[:, cols], rows, cols, scores[hd],
                               v_scr[rows, cols], st_ref[0, hd])
            o_ref[0, rows, cols] = o
            st_ref[0, hd] = st

    _scan_block(tb // SCAN_CHUNK, False, operands, step)


def _hg_bwd_kernel(h_ref, sh_ref, sc_ref, ga_ref, q_ref, v_ref, of_ref, win_ref, lb_ref,
                   nw_ref, wo_ref, lng_ref, lnb_ref, u_ref, s0_ref,
                   hn_ref, st_ref, f_scr, lg3_scr, x_scr, dec_scr, diag_scr, g_scr, y_scr, *, lb_idx):
    @pl.when(pl.program_id(1) == 0)
    def _():
        st_ref[...] = s0_ref[...]

    h = h_ref[0]
    tb = h.shape[0]
    hm = _modulated_bf16(h, sh_ref, sc_ref)
    lb = _lower_bound(lb_ref, lb_idx)
    for sl in _column_tiles(h.shape[1]):
        _forget_gate(_dot(hm, win_ref[:, _shift(sl, HG_COL_F_BWD)]), lb[:, sl], f_scr, lg3_scr, sl)
        g_scr[:, sl] = _silu(_dot(hm, win_ref[:, _shift(sl, HG_COL_GATE)]))
    owner = _pair_owner(True)
    norm_w = nw_ref[...]

    def operands(ci):
        _scan_operands(ci, q_ref, f_scr, lg3_scr, u_ref, x_scr, dec_scr, diag_scr, True)

    def step(ci):
        rows = _chunk_rows(ci)
        dec = dec_scr[ci]
        scores = _chunk_scores(x_scr, diag_scr, rows, owner)
        for hd in range(HG_HEADS):
            cols = slice(hd * HG_DK, (hd + 1) * HG_DK)
            o, st = _scan_step(x_scr, dec[:, cols], rows, cols, scores[hd],
                               v_ref[0, rows, cols], st_ref[0, hd])
            st_ref[0, hd] = st
            o = o + of_ref[0, rows, cols]
            o = o * lax.rsqrt(jnp.mean(o * o, axis=-1, keepdims=True) + RMS_EPS) * norm_w
            y_scr[rows, cols] = (o * g_scr[rows, cols]).astype(BF16)

    _scan_block(tb // SCAN_CHUNK, True, operands, step)
    y = _dot(y_scr[...], wo_ref[...])
    hn_ref[0] = _layer_norm(ALPHA * h + ga_ref[0] * y, lng_ref[...], lnb_ref[...])


def _hgrn_layer(h, mods, w, idx, s0_f, s0_b, tb):
    b, l, d = h.shape
    nb = l // tb
    vec = _per_batch((1, d))
    state = _per_batch((HG_HEADS, HG_DK, HG_DK))
    state_shape = jax.ShapeDtypeStruct((b, HG_HEADS, HG_DK, HG_DK), F32)
    tok_shape = jax.ShapeDtypeStruct(h.shape, F32)
    n_lb = w["lb_f"].shape[0]
    seg = _resident((SCAN_SEGMENTS * SCAN_CHUNK, 3 * SCAN_CHUNK))
    scan_scratch = [pltpu.VMEM((tb, d), F32),
                    pltpu.VMEM((3, tb, d), BF16),
                    pltpu.VMEM((SCAN_OPERANDS, tb, d), BF16),
                    pltpu.VMEM((tb // SCAN_CHUNK, 1, d), F32),
                    pltpu.VMEM((tb, LANES), F32)]

    tok = pl.BlockSpec((1, tb, d), lambda i, j: (i, j, 0))
    o_f, q, v, s_f = pl.pallas_call(
        functools.partial(_hg_fwd_kernel, lb_idx=idx),
        grid=(b, nb),
        in_specs=[tok, vec, vec, _resident_layer((d, HG_IN), idx), _resident((n_lb, d)), seg, state],
        out_specs=[tok, tok, tok, state],
        out_shape=[tok_shape, tok_shape, jax.ShapeDtypeStruct(h.shape, BF16), state_shape],
        scratch_shapes=scan_scratch + [pltpu.VMEM((1, tb, d), F32), pltpu.VMEM((tb, d), BF16)],
        compiler_params=_params("arbitrary", "arbitrary"),
        name="hgrn2_forward_scan",
    )(h, mods["sh1"], mods["sc1"], w["w_in"], w["lb_f"], _segment_matrix(False), s0_f)

    rtok = pl.BlockSpec((1, tb, d), lambda i, j: (i, nb - 1 - j, 0))
    h_new, s_b = pl.pallas_call(
        functools.partial(_hg_bwd_kernel, lb_idx=idx),
        grid=(b, nb),
        in_specs=[rtok, vec, vec, vec, rtok, rtok, rtok,
                  _resident_layer((d, HG_IN), idx), _resident((n_lb, d)),
                  _resident((1, HG_DK)), _resident_layer((d, d), idx), _resident((1, d)),
                  _resident((1, d)), seg, state],
        out_specs=[rtok, state],
        out_shape=[tok_shape, state_shape],
        scratch_shapes=scan_scratch + [pltpu.VMEM((tb, d), F32), pltpu.VMEM((tb, d), BF16)],
        compiler_params=_params("arbitrary", "arbitrary"),
        name="hgrn2_backward_scan_readout",
    )(h, mods["sh1"], mods["sc1"], mods["ga1"], q, v, o_f, w["w_in"], w["lb_b"],
      w["norm_w"], w["w_out"], w["ln_g"], w["ln_b"], _segment_matrix(True), s0_b)
    return h_new, s_f, s_b


def _token_block(l):
    for tb in (512, 256, 128):
        if l % tb == 0:
            return tb
    raise ValueError(f"sequence length {l} is not a multiple of 128")


def kernel(x, c, ctx, c_ctx, ada_w, ada_b, ln_g, ln_b, hg_w_in, hg_lb, hg_norm_w, hg_w_out, sgu_w_in, sgu_ln_g, sgu_ln_b, sgu_w_s, sgu_b_s, sgu_w_out, ffn_w_up, ffn_conv_w, ffn_conv_b, ffn_w_down):
    batch, seq, d = x.shape
    ctx_len = ctx.shape[1]
    assert d == D_MODEL and batch + 1 <= ADA_ROWS
    tb_x, tb_c = _token_block(seq), _token_block(ctx_len)
    assert tb_x % GRID_W == 0 and tb_c == ctx_len

    cc = jnp.concatenate([c, c_ctx[None], jnp.zeros((ADA_ROWS - batch - 1, d), F32)], axis=0)
    ada = _ada_modulation(cc, ada_w, ada_b)
    names = ("sh1", "sc1", "ga1", "sh2", "sc2", "ga2")

    def mods_for(layer, is_ctx):
        m = ada[layer].reshape(ADA_ROWS, 6, 1, d)
        if is_ctx:
            rows = jnp.broadcast_to(m[batch:batch + 1], (batch, 6, 1, d))
        else:
            rows = m[:batch]
        return {n: rows[:, i] for i, n in enumerate(names)}

    hg_w_in_b, hg_w_out_b = hg_w_in.astype(BF16), hg_w_out.astype(BF16)
    sgu_w_in_b, sgu_w_out_b = sgu_w_in.astype(BF16), sgu_w_out.astype(BF16)
    ffn_w_up_b, ffn_w_down_b = ffn_w_up.astype(BF16), ffn_w_down.astype(BF16)
    ffn_conv_b3 = ffn_conv_b[:, None, :]
    zero_state = jnp.zeros((batch, HG_HEADS, HG_DK, HG_DK), F32)
    h, hc = x, ctx
    for layer in range(DEPTH):
        kind, idx = layer % N_MIXERS, layer // N_MIXERS
        ctx_later = any(j % N_MIXERS == 0 for j in range(layer + 1, DEPTH))
        mx = mods_for(layer, False)
        mc = mods_for(layer, True) if (kind == 0 or ctx_later) else None
        ln1 = {"ln_g": ln_g[layer, 0][None], "ln_b": ln_b[layer, 0][None]}
        if kind == 0:
            w = dict(ln1, w_in=hg_w_in_b, lb_f=hg_lb[0], lb_b=hg_lb[1],
                     norm_w=hg_norm_w[idx][None], w_out=hg_w_out_b)
            hc_new, s_f, s_b = _hgrn_layer(hc, mc, w, idx, zero_state, zero_state, tb_c)
            h, _, _ = _hgrn_layer(h, mx, w, idx, s_f, s_b, tb_x)
            if ctx_later:
                hc = hc_new
        else:
            w = dict(ln1, w_in=sgu_w_in_b, v_g=sgu_ln_g[idx][None], v_b=sgu_ln_b[idx][None],
                     w_s=sgu_w_s[idx].astype(BF16), b_s=jnp.repeat(sgu_b_s[idx].T, SGU_GW, axis=1),
                     w_out=sgu_w_out_b)
            h = _sgu_layer(h, mx, w, idx, tb_x)
            if ctx_later:
                hc = _sgu_layer(hc, mc, w, idx, tb_c)
        w = dict(w_up=ffn_w_up_b, conv_w=ffn_conv_w, conv_b=ffn_conv_b3, w_down=ffn_w_down_b,
                 ln_g=ln_g[layer, 1][None], ln_b=ln_b[layer, 1][None])
        h = _ffn_layer(h, mx, w, layer, GRID_W, tb_x)
        if ctx_later:
            hc = _ffn_layer(hc, mc, w, layer, ctx_len, tb_c)
    return h
```

```python
import functools

import jax
import jax.numpy as jnp
import numpy as np
from jax import lax
from jax.experimental import pallas as pl
from jax.experimental.pallas import tpu as pltpu

F32 = jnp.float32
BF16 = jnp.bfloat16

LANES = 128
D_MODEL = 1024
DEPTH = 4
N_MIXERS = 2
GRID_W = 64
HG_DK = LANES
HG_HEADS = D_MODEL // HG_DK
HG_COL_Q, HG_COL_GATE, HG_COL_V, HG_COL_F_FWD, HG_COL_F_BWD = (i * D_MODEL for i in range(5))
HG_IN = 5 * D_MODEL
SCAN_CHUNK = 64
SCAN_LEVELS = 6
SUBLANES = 8
SCAN_MM_LEVELS = SCAN_LEVELS - 1
SCAN_SEGMENTS = 3
SCAN_OPERANDS = SCAN_MM_LEVELS + 2
LOG2_E = 1.4426950408889634
SGU_CHUNK = 128
SGU_WIDTH = 3 * D_MODEL
SGU_GROUPS = 8
SGU_GW = SGU_WIDTH // SGU_GROUPS
SGU_TILE = 2 * SGU_GW
D_FF = 128 * ((8 * D_MODEL // 3 + 127) // 128)
FF_TILE = 256
PROJ_TILE = 256
TAIL_ROWS = 256
ALPHA = (2.0 * DEPTH) ** 0.25
LN_EPS = 1e-5
RMS_EPS = 1e-6
VMEM_LIMIT_BYTES = 56 * 1024 * 1024
ADA_TILE = 1536
ADA_ROWS = 8

_NT = (((1,), (1,)), ((), ()))
_TN = (((0,), (0,)), ((), ()))


def _params(*sem):
    return pltpu.CompilerParams(dimension_semantics=sem, vmem_limit_bytes=VMEM_LIMIT_BYTES)


def _resident(shape):
    zeros = (0,) * len(shape)
    return pl.BlockSpec(shape, lambda *_: zeros, pipeline_mode=pl.Buffered(1))


def _resident_layer(shape, layer):
    zeros = (0,) * len(shape)
    return pl.BlockSpec((None,) + tuple(shape), lambda *_: (layer,) + zeros,
                        pipeline_mode=pl.Buffered(1))


def _per_batch(shape):
    zeros = (0,) * len(shape)
    return pl.BlockSpec((1,) + tuple(shape), lambda b, j: (b,) + zeros)


def _dot(a, b):
    return jnp.dot(a, b, preferred_element_type=F32)


def _silu(x):
    return x * jax.nn.sigmoid(x)


def _gelu(x):
    return 0.5 * x * (1.0 + lax.erf(x * (2.0 ** -0.5)))


def _layer_norm(x, g, b):
    mu = jnp.mean(x, axis=-1, keepdims=True)
    xc = x - mu
    var = jnp.mean(xc * xc, axis=-1, keepdims=True)
    return xc * lax.rsqrt(var + LN_EPS) * g + b


def _project_and_norm(h_ref, ga_ref, act_ref, w_ref, lng_ref, lnb_ref, o_ref):
    tb = h_ref.shape[1]
    step = min(tb, TAIL_ROWS)
    for r in range(0, tb, step):
        rows = slice(r, r + step)
        y = _dot(act_ref[rows, :], w_ref[...])
        o_ref[0, rows, :] = _layer_norm(ALPHA * h_ref[0, rows, :] + ga_ref[0] * y,
                                        lng_ref[...], lnb_ref[...])


def _modulated_bf16(h, sh_ref, sc_ref):
    return (h * (1.0 + sc_ref[0]) + sh_ref[0]).astype(BF16)


def _ada_kernel(c_ref, w_ref, b_ref, o_ref):
    s = _silu(c_ref[...])
    o_ref[0] = jnp.dot(s, w_ref[0], precision=lax.Precision.HIGHEST,
                       preferred_element_type=F32) + b_ref[0]


def _ada_modulation(cc, ada_w, ada_b):
    n = ada_w.shape[-1]
    return pl.pallas_call(
        _ada_kernel,
        grid=(DEPTH, n // ADA_TILE),
        in_specs=[pl.BlockSpec((ADA_ROWS, D_MODEL), lambda l, j: (0, 0)),
                  pl.BlockSpec((1, D_MODEL, ADA_TILE), lambda l, j: (l, 0, j)),
                  pl.BlockSpec((1, 1, ADA_TILE), lambda l, j: (l, 0, j))],
        out_specs=pl.BlockSpec((1, ADA_ROWS, ADA_TILE), lambda l, j: (l, 0, j)),
        out_shape=jax.ShapeDtypeStruct((DEPTH, ADA_ROWS, n), F32),
        compiler_params=_params("arbitrary", "arbitrary"),
        name="ada_modulation",
    )(cc, ada_w, ada_b.reshape(DEPTH, 1, n))


def _ffn_kernel(h_ref, sh_ref, sc_ref, ga_ref, wu_ref, cw_ref, cb_ref, wd_ref, lng_ref, lnb_ref,
                o_ref, act_ref, *, row_len):
    h = h_ref[0]
    tb = h.shape[0]
    hm = _modulated_bf16(h, sh_ref, sc_ref)
    col = lax.broadcasted_iota(jnp.int32, (tb, FF_TILE), 0) % row_len
    has_prev = col != 0
    has_next = col != row_len - 1

    def conv(sl):
        y = _dot(hm, wu_ref[:, sl])
        y_prev = jnp.where(has_prev, pltpu.roll(y, 1, 0), 0.0)
        y_next = jnp.where(has_next, pltpu.roll(y, tb - 1, 0), 0.0)
        return (y_prev * cw_ref[0:1, sl] + y * cw_ref[1:2, sl] + y_next * cw_ref[2:3, sl]
                + cb_ref[:, sl])

    for j in range(D_FF // FF_TILE):
        sl = slice(j * FF_TILE, (j + 1) * FF_TILE)
        a = conv(sl)
        g = conv(slice(D_FF + sl.start, D_FF + sl.stop))
        act_ref[:, sl] = (_silu(g) * a).astype(BF16)
    _project_and_norm(h_ref, ga_ref, act_ref, wd_ref, lng_ref, lnb_ref, o_ref)


def _ffn_layer(h, mods, w, layer, row_len, tb):
    b, l, d = h.shape
    tok = pl.BlockSpec((1, tb, d), lambda i, j: (i, j, 0))
    vec = _per_batch((1, d))
    return pl.pallas_call(
        functools.partial(_ffn_kernel, row_len=row_len),
        grid=(b, l // tb),
        in_specs=[tok, vec, vec, vec,
                  _resident_layer((d, 2 * D_FF), layer), _resident_layer((3, 2 * D_FF), layer),
                  _resident_layer((1, 2 * D_FF), layer), _resident_layer((D_FF, d), layer),
                  _resident((1, d)), _resident((1, d))],
        out_specs=tok,
        out_shape=jax.ShapeDtypeStruct(h.shape, F32),
        scratch_shapes=[pltpu.VMEM((tb, D_FF), BF16)],
        compiler_params=_params("arbitrary", "arbitrary"),
        name="conv_ffn",
    )(h, mods["sh2"], mods["sc2"], mods["ga2"], w["w_up"], w["conv_w"], w["conv_b"], w["w_down"],
      w["ln_g"], w["ln_b"])


def _sgu_kernel(h_ref, sh_ref, sc_ref, ga_ref, win_ref, vg_ref, vb_ref, ws_ref, bs_ref, wout_ref,
                lng_ref, lnb_ref, o_ref, v_ref, p_ref):
    h = h_ref[0]
    tb = h.shape[0]
    hm = _modulated_bf16(h, sh_ref, sc_ref)
    tiles = [slice(j, j + SGU_TILE) for j in range(0, SGU_WIDTH, SGU_TILE)]

    total = jnp.zeros((tb, 1), F32)
    for sl in tiles:
        z = _gelu(_dot(hm, win_ref[:, SGU_WIDTH + sl.start:SGU_WIDTH + sl.stop]))
        v_ref[:, sl] = z
        total = total + jnp.sum(z, axis=-1, keepdims=True)
    mu = total * (1.0 / SGU_WIDTH)
    sq = jnp.zeros((tb, 1), F32)
    for sl in tiles:
        zc = v_ref[:, sl] - mu
        sq = sq + jnp.sum(zc * zc, axis=-1, keepdims=True)
    rstd = lax.rsqrt(sq * (1.0 / SGU_WIDTH) + LN_EPS)

    for t, sl in enumerate(tiles):
        vn = ((v_ref[:, sl] - mu) * rstd * vg_ref[:, sl] + vb_ref[:, sl]).astype(BF16)
        u = _gelu(_dot(hm, win_ref[:, sl]))
        for gi in range(SGU_TILE // SGU_GW):
            g = t * (SGU_TILE // SGU_GW) + gi
            lo = slice(gi * SGU_GW, (gi + 1) * SGU_GW)
            gl = slice(g * SGU_GW, (g + 1) * SGU_GW)
            for n in range(tb // SGU_CHUNK):
                rows = slice(n * SGU_CHUNK, (n + 1) * SGU_CHUNK)
                mixed = _dot(ws_ref[g], vn[rows, lo]) + bs_ref[:, gl]
                p_ref[rows, gl] = (u[rows, lo] * mixed).astype(BF16)
    _project_and_norm(h_ref, ga_ref, p_ref, wout_ref, lng_ref, lnb_ref, o_ref)


def _sgu_layer(h, mods, w, idx, tb):
    b, l, d = h.shape
    tok = pl.BlockSpec((1, tb, d), lambda i, j: (i, j, 0))
    vec = _per_batch((1, d))
    return pl.pallas_call(
        _sgu_kernel,
        grid=(b, l // tb),
        in_specs=[tok, vec, vec, vec,
                  _resident_layer((d, 2 * SGU_WIDTH), idx), _resident((1, SGU_WIDTH)),
                  _resident((1, SGU_WIDTH)),
                  _resident((SGU_GROUPS, SGU_CHUNK, SGU_CHUNK)), _resident((SGU_CHUNK, SGU_WIDTH)),
                  _resident_layer((SGU_WIDTH, d), idx), _resident((1, d)), _resident((1, d))],
        out_specs=tok,
        out_shape=jax.ShapeDtypeStruct(h.shape, F32),
        scratch_shapes=[pltpu.VMEM((tb, SGU_WIDTH), F32), pltpu.VMEM((tb, SGU_WIDTH), BF16)],
        compiler_params=_params("arbitrary", "arbitrary"),
        name="chunk_sgu",
    )(h, mods["sh1"], mods["sc1"], mods["ga1"], w["w_in"], w["v_g"], w["v_b"], w["w_s"], w["b_s"],
      w["w_out"], w["ln_g"], w["ln_b"])


def _lower_bound(lb_ref, idx):
    x = lb_ref[...]
    e = jnp.exp(x - jnp.max(x, axis=0, keepdims=True))
    p = e / jnp.sum(e, axis=0, keepdims=True)
    acc = p[0:1]
    for r in range(1, idx + 1):
        acc = acc + p[r:r + 1]
    return acc - p[0:1]


def _level_half(i):
    return (SCAN_CHUNK // 2) >> i


def _segment_matrix(rev):
    c = SCAN_CHUNK
    t = np.arange(c)[:, None]
    r = np.arange(c)[None, :]
    blocks = [r >= t] if rev else [r <= t]
    for i in range(SCAN_MM_LEVELS):
        m = _level_half(i)
        if m >= SUBLANES:
            continue
        upper = (t % (2 * m)) >= m
        if rev:
            ref = t - t % (2 * m) + m
            blocks.append(np.where(upper, (r >= ref) & (r < t), (r >= t) & (r < ref)))
        else:
            ref = t - t % (2 * m) + m - 1
            blocks.append(np.where(upper, (r > ref) & (r <= t), (r > t) & (r <= ref)))
    assert len(blocks) == SCAN_SEGMENTS
    u = np.concatenate(blocks, axis=0).astype(np.float32)
    return jnp.asarray(np.tile(u, (1, 3)), BF16)


def _pair_owner(rev):
    c = SCAN_CHUNK
    t = lax.broadcasted_iota(jnp.int32, (c, c), 0)
    s = lax.broadcasted_iota(jnp.int32, (c, c), 1)
    x = t ^ s
    owner = jnp.where(t == s, SCAN_LEVELS, -1)
    for i in range(SCAN_LEVELS):
        m = (c // 2) >> i
        owner = jnp.where((x >= m) & (x < 2 * m), i, owner)
    return jnp.where((t <= s) if rev else (t >= s), owner, -1)


def _query_rows(rev):
    row = lax.broadcasted_iota(jnp.int32, (SCAN_CHUNK, D_MODEL), 0)
    masks = []
    for i in range(SCAN_MM_LEVELS):
        upper = (row & ((SCAN_CHUNK // 2) >> i)) != 0
        masks.append(jnp.logical_not(upper) if rev else upper)
    return masks


def _shift(sl, offset):
    return slice(sl.start + offset, sl.stop + offset)


def _column_tiles(width):
    return [slice(j, j + PROJ_TILE) for j in range(0, width, PROJ_TILE)]


def _forget_gate(z, lb, f_ref, lg3_ref, sl):
    f = lb + (1.0 - lb) * jax.nn.sigmoid(z)
    f_ref[:, sl] = f
    lg = jnp.log(f) * LOG2_E
    hi = lg.astype(BF16)
    rest = lg - hi.astype(F32)
    mid = rest.astype(BF16)
    lg3_ref[0, :, sl] = hi
    lg3_ref[1, :, sl] = mid
    lg3_ref[2, :, sl] = (rest - mid.astype(F32)).astype(BF16)


def _slab_level(gsum, q, k, m, rev):
    parts = []
    for start in range(0, SCAN_CHUNK, 2 * m):
        lo, hi = slice(start, start + m), slice(start + m, start + 2 * m)
        if rev:
            ref = gsum[start + m:start + m + 1]
            parts += [q[lo] * jnp.exp2(gsum[lo] - ref), k[hi] * jnp.exp2(ref - gsum[hi])]
        else:
            ref = gsum[start + m - 1:start + m]
            parts += [k[lo] * jnp.exp2(ref - gsum[lo]), q[hi] * jnp.exp2(gsum[hi] - ref)]
    return jnp.concatenate(parts, axis=0)


def _scan_operands(ci, q_ref, f_ref, lg3_ref, u_ref, x_ref, dec_ref, diag_ref, rev):
    c = SCAN_CHUNK
    query_rows = _query_rows(rev)
    lane = lax.broadcasted_iota(jnp.int32, (c, LANES), 1)
    rows = _chunk_rows(ci)
    lg3 = jnp.concatenate([lg3_ref[0, rows, :], lg3_ref[1, rows, :], lg3_ref[2, rows, :]], axis=0)
    seg = _dot(u_ref[...], lg3)
    gsum = seg[0:c]
    q = q_ref[0, rows, :]
    f = f_ref[rows, :]
    k = 1.0 - f
    block = 1
    for i in range(SCAN_MM_LEVELS):
        if _level_half(i) >= SUBLANES:
            x = _slab_level(gsum, q, k, _level_half(i), rev)
        else:
            x = jnp.where(query_rows[i], q, k) * jnp.exp2(seg[block * c:(block + 1) * c])
            block += 1
        x_ref[i, rows, :] = x.astype(BF16)
    g_end = gsum[0:1] if rev else gsum[c - 1:c]
    x_ref[SCAN_MM_LEVELS, rows, :] = (q * jnp.exp2(gsum)).astype(BF16)
    x_ref[SCAN_MM_LEVELS + 1, rows, :] = (k * jnp.exp2(g_end - gsum)).astype(BF16)
    dec_ref[ci] = jnp.exp2(g_end)
    k_before = pltpu.roll(k, c - 1 if rev else 1, 0)
    diag = jnp.zeros((c, LANES), F32)
    for j, prod in enumerate((q * k, q * f * k_before)):
        for hd in range(HG_HEADS):
            head_sum = jnp.sum(prod[:, hd * HG_DK:(hd + 1) * HG_DK], axis=-1, keepdims=True)
            diag = jnp.where(lane == j * HG_HEADS + hd, head_sum, diag)
    diag_ref[rows, :] = diag


def _chunk_rows(ci):
    return slice(ci * SCAN_CHUNK, (ci + 1) * SCAN_CHUNK)


def _scan_block(n_chunks, rev, operands, step):
    order = list(range(n_chunks))[::-1] if rev else list(range(n_chunks))
    operands(order[0])
    for i, ci in enumerate(order):
        if i + 1 < n_chunks:
            operands(order[i + 1])
        step(ci)


def _chunk_scores(x_ref, diag_ref, rows, owner):
    diag = diag_ref[rows, :]
    scores = []
    for hd in range(HG_HEADS):
        cols = slice(hd * HG_DK, (hd + 1) * HG_DK)
        s = jnp.where(owner == SCAN_LEVELS, diag[:, hd:hd + 1],
                      jnp.where(owner == SCAN_MM_LEVELS,
                                diag[:, HG_HEADS + hd:HG_HEADS + hd + 1], 0.0))
        for i in range(SCAN_MM_LEVELS):
            x = x_ref[i, rows, cols]
            s = jnp.where(owner == i, lax.dot_general(x, x, _NT, preferred_element_type=F32), s)
        scores.append(s.astype(BF16))
    return scores


def _scan_step(x_ref, dec, rows, cols, scores, vb, st):
    o = _dot(scores, vb) + _dot(x_ref[SCAN_MM_LEVELS, rows, cols], st.astype(BF16))
    dec_rows = jnp.broadcast_to(dec, (HG_DK, HG_DK)).T
    st_new = st * dec_rows + lax.dot_general(x_ref[SCAN_MM_LEVELS + 1, rows, cols], vb, _TN,
                                             preferred_element_type=F32)
    return o, st_new


def _hg_fwd_kernel(h_ref, sh_ref, sc_ref, win_ref, lb_ref, u_ref, s0_ref,
                   o_ref, q_ref, v_ref, st_ref, f_scr, lg3_scr, x_scr, dec_scr, diag_scr, *, lb_idx):
    @pl.when(pl.program_id(1) == 0)
    def _():
        st_ref[...] = s0_ref[...]

    h = h_ref[0]
    tb = h.shape[0]
    hm = _modulated_bf16(h, sh_ref, sc_ref)
    lb = _lower_bound(lb_ref, lb_idx)
    for sl in _column_tiles(h.shape[1]):
        _forget_gate(_dot(hm, win_ref[:, _shift(sl, HG_COL_F_FWD)]), lb[:, sl], f_scr, lg3_scr, sl)
        q_ref[0, :, sl] = _silu(_dot(hm, win_ref[:, _shift(sl, HG_COL_Q)]))
        v_ref[0, :, sl] = _dot(hm, win_ref[:, _shift(sl, HG_COL_V)]).astype(BF16)
    owner = _pair_owner(False)

    def operands(ci):
        _scan_operands(ci, q_ref, f_scr, lg3_scr, u_ref, x_scr, dec_scr, diag_scr, False)

    def step(ci):
        rows = _chunk_rows(ci)
        dec = dec_scr[ci]
        scores = _chunk_scores(x_scr, diag_scr, rows, owner)
        for hd in range(HG_HEADS):
            cols = slice(hd * HG_DK, (hd + 1) * HG_DK)
            o, st = _scan_step(x_scr, dec[:, cols], rows, cols, scores[hd],
                               v_ref[0, rows, cols], st_ref[0, hd])
            o_ref[0, rows, cols] = o
            st_ref[0, hd] = st

    _scan_block(tb // SCAN_CHUNK, False, operands, step)


def _hg_bwd_kernel(h_ref, sh_ref, sc_ref, ga_ref, q_ref, v_ref, of_ref, win_ref, lb_ref,
                   nw_ref, wo_ref, lng_ref, lnb_ref, u_ref, s0_ref,
                   hn_ref, st_ref, f_scr, lg3_scr, x_scr, dec_scr, diag_scr, g_scr, y_scr, *, lb_idx):
    @pl.when(pl.program_id(1) == 0)
    def _():
        st_ref[...] = s0_ref[...]

    h = h_ref[0]
    tb = h.shape[0]
    hm = _modulated_bf16(h, sh_ref, sc_ref)
    lb = _lower_bound(lb_ref, lb_idx)
    for sl in _column_tiles(h.shape[1]):
        _forget_gate(_dot(hm, win_ref[:, _shift(sl, HG_COL_F_BWD)]), lb[:, sl], f_scr, lg3_scr, sl)
        g_scr[:, sl] = _silu(_dot(hm, win_ref[:, _shift(sl, HG_COL_GATE)]))
    owner = _pair_owner(True)
    norm_w = nw_ref[...]

    def operands(ci):
        _scan_operands(ci, q_ref, f_scr, lg3_scr, u_ref, x_scr, dec_scr, diag_scr, True)

    def step(ci):
        rows = _chunk_rows(ci)
        dec = dec_scr[ci]
        scores = _chunk_scores(x_scr, diag_scr, rows, owner)
        for hd in range(HG_HEADS):
            cols = slice(hd * HG_DK, (hd + 1) * HG_DK)
            o, st = _scan_step(x_scr, dec[:, cols], rows, cols, scores[hd],
                               v_ref[0, rows, cols], st_ref[0, hd])
            st_ref[0, hd] = st
            o = o + of_ref[0, rows, cols]
            o = o * lax.rsqrt(jnp.mean(o * o, axis=-1, keepdims=True) + RMS_EPS) * norm_w
            y_scr[rows, cols] = (o * g_scr[rows, cols]).astype(BF16)

    _scan_block(tb // SCAN_CHUNK, True, operands, step)
    _project_and_norm(h_ref, ga_ref, y_scr, wo_ref, lng_ref, lnb_ref, hn_ref)


def _hgrn_layer(h, mods, w, idx, s0_f, s0_b, tb):
    b, l, d = h.shape
    nb = l // tb
    vec = _per_batch((1, d))
    state = _per_batch((HG_HEADS, HG_DK, HG_DK))
    state_shape = jax.ShapeDtypeStruct((b, HG_HEADS, HG_DK, HG_DK), F32)
    tok_shape = jax.ShapeDtypeStruct(h.shape, F32)
    n_lb = w["lb_f"].shape[0]
    seg = _resident((SCAN_SEGMENTS * SCAN_CHUNK, 3 * SCAN_CHUNK))
    scan_scratch = [pltpu.VMEM((tb, d), F32),
                    pltpu.VMEM((3, tb, d), BF16),
                    pltpu.VMEM((SCAN_OPERANDS, tb, d), BF16),
                    pltpu.VMEM((tb // SCAN_CHUNK, 1, d), F32),
                    pltpu.VMEM((tb, LANES), F32)]

    tok = pl.BlockSpec((1, tb, d), lambda i, j: (i, j, 0))
    o_f, q, v, s_f = pl.pallas_call(
        functools.partial(_hg_fwd_kernel, lb_idx=idx),
        grid=(b, nb),
        in_specs=[tok, vec, vec, _resident_layer((d, HG_IN), idx), _resident((n_lb, d)), seg, state],
        out_specs=[tok, tok, tok, state],
        out_shape=[tok_shape, tok_shape, jax.ShapeDtypeStruct(h.shape, BF16), state_shape],
        scratch_shapes=scan_scratch,
        compiler_params=_params("arbitrary", "arbitrary"),
        name="hgrn2_forward_scan",
    )(h, mods["sh1"], mods["sc1"], w["w_in"], w["lb_f"], _segment_matrix(False), s0_f)

    rtok = pl.BlockSpec((1, tb, d), lambda i, j: (i, nb - 1 - j, 0))
    h_new, s_b = pl.pallas_call(
        functools.partial(_hg_bwd_kernel, lb_idx=idx),
        grid=(b, nb),
        in_specs=[rtok, vec, vec, vec, rtok, rtok, rtok,
                  _resident_layer((d, HG_IN), idx), _resident((n_lb, d)),
                  _resident((1, HG_DK)), _resident_layer((d, d), idx), _resident((1, d)),
                  _resident((1, d)), seg, state],
        out_specs=[rtok, state],
        out_shape=[tok_shape, state_shape],
        scratch_shapes=scan_scratch + [pltpu.VMEM((tb, d), F32), pltpu.VMEM((tb, d), BF16)],
        compiler_params=_params("arbitrary", "arbitrary"),
        name="hgrn2_backward_scan_readout",
    )(h, mods["sh1"], mods["sc1"], mods["ga1"], q, v, o_f, w["w_in"], w["lb_b"],
      w["norm_w"], w["w_out"], w["ln_g"], w["ln_b"], _segment_matrix(True), s0_b)
    return h_new, s_f, s_b


def _token_block(l):
    for tb in (512, 256, 128):
        if l % tb == 0:
            return tb
    raise ValueError(f"sequence length {l} is not a multiple of 128")


def kernel(x, c, ctx, c_ctx, ada_w, ada_b, ln_g, ln_b, hg_w_in, hg_lb, hg_norm_w, hg_w_out, sgu_w_in, sgu_ln_g, sgu_ln_b, sgu_w_s, sgu_b_s, sgu_w_out, ffn_w_up, ffn_conv_w, ffn_conv_b, ffn_w_down):
    batch, seq, d = x.shape
    ctx_len = ctx.shape[1]
    assert d == D_MODEL and batch + 1 <= ADA_ROWS
    tb_x, tb_c = _token_block(seq), _token_block(ctx_len)
    assert tb_x % GRID_W == 0 and tb_c == ctx_len

    cc = jnp.concatenate([c, c_ctx[None], jnp.zeros((ADA_ROWS - batch - 1, d), F32)], axis=0)
    ada = _ada_modulation(cc, ada_w, ada_b)
    names = ("sh1", "sc1", "ga1", "sh2", "sc2", "ga2")

    def mods_for(layer, is_ctx):
        m = ada[layer].reshape(ADA_ROWS, 6, 1, d)
        if is_ctx:
            rows = jnp.broadcast_to(m[batch:batch + 1], (batch, 6, 1, d))
        else:
            rows = m[:batch]
        return {n: rows[:, i] for i, n in enumerate(names)}

    hg_w_in_b, hg_w_out_b = hg_w_in.astype(BF16), hg_w_out.astype(BF16)
    sgu_w_in_b, sgu_w_out_b = sgu_w_in.astype(BF16), sgu_w_out.astype(BF16)
    ffn_w_up_b, ffn_w_down_b = ffn_w_up.astype(BF16), ffn_w_down.astype(BF16)
    ffn_conv_b3 = ffn_conv_b[:, None, :]
    zero_state = jnp.zeros((batch, HG_HEADS, HG_DK, HG_DK), F32)
    h, hc = x, ctx
    for layer in range(DEPTH):
        kind, idx = layer % N_MIXERS, layer // N_MIXERS
        ctx_later = any(j % N_MIXERS == 0 for j in range(layer + 1, DEPTH))
        mx = mods_for(layer, False)
        mc = mods_for(layer, True) if (kind == 0 or ctx_later) else None
        ln1 = {"ln_g": ln_g[layer, 0][None], "ln_b": ln_b[layer, 0][None]}
        if kind == 0:
            w = dict(ln1, w_in=hg_w_in_b, lb_f=hg_lb[0], lb_b=hg_lb[1],
                     norm_w=hg_norm_w[idx][None], w_out=hg_w_out_b)
            hc_new, s_f, s_b = _hgrn_layer(hc, mc, w, idx, zero_state, zero_state, tb_c)
            h, _, _ = _hgrn_layer(h, mx, w, idx, s_f, s_b, tb_x)
            if ctx_later:
                hc = hc_new
        else:
            w = dict(ln1, w_in=sgu_w_in_b, v_g=sgu_ln_g[idx][None], v_b=sgu_ln_b[idx][None],
                     w_s=sgu_w_s[idx].astype(BF16), b_s=jnp.repeat(sgu_b_s[idx].T, SGU_GW, axis=1),
                     w_out=sgu_w_out_b)
            h = _sgu_layer(h, mx, w, idx, tb_x)
            if ctx_later:
                hc = _sgu_layer(hc, mc, w, idx, tb_c)
        w = dict(w_up=ffn_w_up_b, conv_w=ffn_conv_w, conv_b=ffn_conv_b3, w_down=ffn_w_down_b,
                 ln_g=ln_g[layer, 1][None], ln_b=ln_b[layer, 1][None])
        h = _ffn_layer(h, mx, w, layer, GRID_W, tb_x)
        if ctx_later:
            hc = _ffn_layer(hc, mc, w, layer, ctx_len, tb_c)
    return h
```
